```python
import math
import jax, jax.numpy as jnp
from jax import lax
import numpy as np

D_MODEL = 1024
BATCH = 2
SEQ = 8192
DEPTH = 2
DEC_BATCH = 128
DEC_SEQ = 4
PAST_LEN = 2048
PAGE_SIZE = 128

N_A_LAYERS = DEPTH // 2
N_B_LAYERS = DEPTH - N_A_LAYERS
GDN_HEADS = D_MODEL // 128
GDN_DK = 128
GDN_DV = 128
GDN_QK_DIM = GDN_HEADS * GDN_DK
GDN_V_DIM = GDN_HEADS * GDN_DV
CONV_W = 4
CONV_DIM = 2 * GDN_QK_DIM + GDN_V_DIM
GDN_IN_DIM = CONV_DIM + GDN_V_DIM + 2 * GDN_HEADS
CHUNK = 64
SB_HEADS = D_MODEL // 128
SB_HEAD_DIM = 128
SB_DIM = SB_HEADS * SB_HEAD_DIM
Q_BLOCK = 128
SB_BIAS_INIT = -6.0
D_FF = ((8 * D_MODEL // 3 + 127) // 128) * 128
EPS = 1e-6

kernel_name = 'yoco_gdn_stickbreaking_macaron_step'


def rms_norm(x, w):
    xf = x.astype(jnp.float32)
    y = xf * lax.rsqrt(jnp.mean(xf * xf, axis=-1, keepdims=True) + EPS)
    return (y * w.astype(jnp.float32)).astype(x.dtype)


def l2_normalize(x):
    return x * lax.rsqrt(jnp.sum(x * x, axis=-1, keepdims=True) + EPS)


def swiglu(h, w_gu, w_down):
    gate, up = jnp.split(h @ w_gu, 2, axis=-1)
    return (jax.nn.silu(gate) * up) @ w_down


def gated_delta_rule(q, k, v, g, beta, s0):
    bsz, L, _, _ = q.shape
    c = min(CHUNK, L)
    n_chunk = -(-L // c)
    pad = n_chunk * c - L

    def to_chunks(t):
        t = jnp.pad(t, [(0, 0), (0, pad)] + [(0, 0)] * (t.ndim - 2))
        t = t.reshape((bsz, n_chunk, c) + t.shape[2:])
        return jnp.moveaxis(jnp.moveaxis(t, 1, 0), 3, 2)

    qc, kc, vc, gc, bc = (to_chunks(t) for t in (q, k, v, g, beta))
    gcum = jnp.cumsum(gc, axis=-1)
    idx = jnp.arange(c)
    incl = idx[:, None] >= idx[None, :]
    strict = idx[:, None] > idx[None, :]
    decay = jnp.exp(jnp.where(incl, gcum[..., :, None] - gcum[..., None, :], -jnp.inf))
    kk = jnp.einsum('nbhid,nbhjd->nbhij', kc, kc)
    m = jnp.where(strict, bc[..., :, None] * kk * decay, 0.0)
    a_mat = m + jnp.eye(c, dtype=m.dtype)
    gam = jnp.exp(gcum)
    w_mat = lax.linalg.triangular_solve(a_mat, (bc * gam)[..., None] * kc, left_side=True, lower=True, unit_diagonal=True)
    u_mat = lax.linalg.triangular_solve(a_mat, bc[..., None] * vc, left_side=True, lower=True, unit_diagonal=True)
    qk = jnp.einsum('nbhid,nbhjd->nbhij', qc, kc) * decay
    q_g = qc * gam[..., None]
    k_tail = kc * jnp.exp(gcum[..., -1:] - gcum)[..., None]
    g_last = jnp.exp(gcum[..., -1])

    def step(s, inp):
        w_i, u_i, qk_i, qg_i, kt_i, gl_i = inp
        u = u_i - jnp.einsum('bhcd,bhde->bhce', w_i, s)
        o = jnp.einsum('bhcd,bhde->bhce', qg_i, s) + jnp.einsum('bhij,bhje->bhie', qk_i, u)
        s = gl_i[..., None, None] * s + jnp.einsum('bhcd,bhce->bhde', kt_i, u)
        return s, o

    s_final, o = lax.scan(step, s0, (w_mat, u_mat, qk, q_g, k_tail, g_last))
    o = o.transpose(1, 0, 3, 2, 4).reshape(bsz, n_chunk * c, GDN_HEADS, -1)[:, :L]
    return o, s_final


def gdn_mixer(h, conv_buf, s0, w_in, conv_w, a_log, dt_bias, o_norm, w_out):
    bsz, L, _ = h.shape
    proj = h @ w_in
    qkv = proj[..., :CONV_DIM]
    z = proj[..., CONV_DIM:CONV_DIM + GDN_V_DIM].reshape(bsz, L, GDN_HEADS, GDN_DV)
    b_logit = proj[..., CONV_DIM + GDN_V_DIM:CONV_DIM + GDN_V_DIM + GDN_HEADS]
    a_logit = proj[..., CONV_DIM + GDN_V_DIM + GDN_HEADS:]
    xcat = jnp.concatenate([conv_buf.astype(qkv.dtype), qkv], axis=1)
    conv = jax.nn.silu(sum(xcat[:, w:w + L] * conv_w[w] for w in range(CONV_W)))
    new_buf = xcat[:, L:]
    q = conv[..., :GDN_QK_DIM].reshape(bsz, L, GDN_HEADS, GDN_DK).astype(jnp.float32)
    k = conv[..., GDN_QK_DIM:2 * GDN_QK_DIM].reshape(bsz, L, GDN_HEADS, GDN_DK).astype(jnp.float32)
    v = conv[..., 2 * GDN_QK_DIM:].reshape(bsz, L, GDN_HEADS, GDN_DV).astype(jnp.float32)
    q = l2_normalize(q) * (GDN_DK ** -0.5)
    k = l2_normalize(k)
    beta = jax.nn.sigmoid(b_logit.astype(jnp.float32))
    g = -jnp.exp(a_log.astype(jnp.float32)) * jax.nn.softplus(a_logit.astype(jnp.float32) + dt_bias.astype(jnp.float32))
    o, s_new = gated_delta_rule(q, k, v, g, beta, s0.astype(jnp.float32))
    o = rms_norm(o, o_norm) * jax.nn.silu(z.astype(jnp.float32))
    out = o.reshape(bsz, L, GDN_V_DIM).astype(h.dtype) @ w_out
    return out, new_buf, s_new


def shared_kv(x, kv_norm, w_kv, k_norm):
    bsz, L, _ = x.shape
    kv = rms_norm(x, kv_norm) @ w_kv
    k = rms_norm(kv[..., :SB_DIM].reshape(bsz, L, SB_HEADS, SB_HEAD_DIM), k_norm)
    v = kv[..., SB_DIM:].reshape(bsz, L, SB_HEADS, SB_HEAD_DIM)
    return k, v


def stick_breaking_attention(q, q_pos, k_segs, v_segs, logit_bias):
    bsz, L, H, d = q.shape
    seg_lens = [s.shape[1] for s in k_segs]
    k_pos = jnp.arange(sum(seg_lens))
    blk = min(Q_BLOCK, L)
    n_blk = -(-L // blk)
    pad = n_blk * blk - L
    q_b = jnp.pad(q, ((0, 0), (0, pad), (0, 0), (0, 0))).reshape(bsz, n_blk, blk, H, d).transpose(1, 0, 2, 3, 4)
    p_b = jnp.pad(q_pos, (0, pad), constant_values=-1).reshape(n_blk, blk)
    scale = d ** -0.5
    bias = logit_bias.astype(jnp.float32)[None, :, None, None]

    def attend_block(args):
        q_i, pos_i = args
        z = jnp.concatenate([jnp.einsum('bqhd,bkhd->bhqk', q_i, k_s).astype(jnp.float32) for k_s in k_segs], axis=-1) * scale + bias
        earlier = k_pos[None, :] < pos_i[:, None]
        log_keep = jnp.where(earlier, jax.nn.log_sigmoid(-z), 0.0)
        log_after = lax.cumsum(log_keep, axis=3, reverse=True) - log_keep
        weight = jnp.exp(jnp.where(earlier, jax.nn.log_sigmoid(z) + log_after, -jnp.inf))
        out = 0
        off = 0
        for v_s, n in zip(v_segs, seg_lens):
            out = out + jnp.einsum('bhqk,bkhd->bqhd', weight[..., off:off + n].astype(v_s.dtype), v_s)
            off += n
        return out

    o = lax.map(attend_block, (q_b, p_b))
    return o.transpose(1, 0, 2, 3, 4).reshape(bsz, n_blk * blk, H, d)[:, :L]


def sb_mixer(h, k_segs, v_segs, past_len, w_q, q_norm, logit_bias, w_out):
    bsz, L, _ = h.shape
    q = rms_norm((h @ w_q).reshape(bsz, L, SB_HEADS, SB_HEAD_DIM), q_norm)
    q_pos = past_len + jnp.arange(L, dtype=jnp.int32)
    o = stick_breaking_attention(q, q_pos, k_segs, v_segs, logit_bias)
    return o.reshape(bsz, L, SB_DIM) @ w_out


def setup_inputs(seed: int = 0) -> dict:
    key = jax.random.key(seed)
    keys = jax.random.split(key, 40)
    counter = [0]

    def nk():
        counter[0] += 1
        return keys[counter[0] - 1]

    def nrm(shape, scale=1.0):
        return jax.random.normal(nk(), shape, jnp.float32) * scale

    def gain(shape):
        return 1.0 + 0.02 * nrm(shape)

    n_pages = PAST_LEN // PAGE_SIZE
    n_phys = (DEC_BATCH * n_pages * 5) // 4
    x_prompt = nrm((BATCH, SEQ, D_MODEL))
    x_sample = nrm((DEC_BATCH, DEC_SEQ, D_MODEL))
    state_gdn = nrm((N_A_LAYERS, DEC_BATCH, GDN_HEADS, GDN_DK, GDN_DV), 0.5)
    state_conv = nrm((N_A_LAYERS, DEC_BATCH, CONV_W - 1, CONV_DIM))
    cache_k = nrm((n_phys, PAGE_SIZE, SB_HEADS, SB_HEAD_DIM))
    cache_v = nrm((n_phys, PAGE_SIZE, SB_HEADS, SB_HEAD_DIM))
    page_table = jax.random.permutation(nk(), n_phys)[:DEC_BATCH * n_pages].reshape(DEC_BATCH, n_pages).astype(jnp.int32)
    ffn1_norm = gain((DEPTH, D_MODEL))
    ffn1_w_gu = nrm((DEPTH, D_MODEL, 2 * D_FF), D_MODEL ** -0.5)
    ffn1_w_down = nrm((DEPTH, D_FF, D_MODEL), D_FF ** -0.5)
    ffn2_norm = gain((DEPTH, D_MODEL))
    ffn2_w_gu = nrm((DEPTH, D_MODEL, 2 * D_FF), D_MODEL ** -0.5)
    ffn2_w_down = nrm((DEPTH, D_FF, D_MODEL), D_FF ** -0.5)
    mix_norm = gain((DEPTH, D_MODEL))
    gdn_w_in = nrm((N_A_LAYERS, D_MODEL, GDN_IN_DIM), D_MODEL ** -0.5)
    gdn_conv_w = nrm((N_A_LAYERS, CONV_W, CONV_DIM), CONV_W ** -0.5)
    gdn_a_log = jnp.log(jax.random.uniform(nk(), (N_A_LAYERS, GDN_HEADS), jnp.float32, 1.0, 16.0))
    dt = jnp.exp(jax.random.uniform(nk(), (N_A_LAYERS, GDN_HEADS), jnp.float32, math.log(1e-3), math.log(1e-1)))
    gdn_dt_bias = dt + jnp.log(-jnp.expm1(-dt))
    gdn_o_norm = gain((N_A_LAYERS, GDN_DV))
    gdn_w_out = nrm((N_A_LAYERS, GDN_V_DIM, D_MODEL), GDN_V_DIM ** -0.5)
    kv_norm = gain((D_MODEL,))
    w_kv = nrm((D_MODEL, 2 * SB_DIM), D_MODEL ** -0.5)
    k_norm = gain((SB_HEAD_DIM,))
    sb_w_q = nrm((N_B_LAYERS, D_MODEL, SB_DIM), D_MODEL ** -0.5)
    sb_q_norm = gain((N_B_LAYERS, SB_HEAD_DIM))
    sb_logit_bias = SB_BIAS_INIT + 0.1 * nrm((N_B_LAYERS, SB_HEADS))
    sb_w_out = nrm((N_B_LAYERS, SB_DIM, D_MODEL), SB_DIM ** -0.5)
    return {'x_prompt': x_prompt, 'x_sample': x_sample, 'state_gdn': state_gdn, 'state_conv': state_conv,
            'cache_k': cache_k, 'cache_v': cache_v, 'page_table': page_table,
            'ffn1_norm': ffn1_norm, 'ffn1_w_gu': ffn1_w_gu, 'ffn1_w_down': ffn1_w_down,
            'ffn2_norm': ffn2_norm, 'ffn2_w_gu': ffn2_w_gu, 'ffn2_w_down': ffn2_w_down, 'mix_norm': mix_norm,
            'gdn_w_in': gdn_w_in, 'gdn_conv_w': gdn_conv_w, 'gdn_a_log': gdn_a_log, 'gdn_dt_bias': gdn_dt_bias,
            'gdn_o_norm': gdn_o_norm, 'gdn_w_out': gdn_w_out, 'kv_norm': kv_norm, 'w_kv': w_kv, 'k_norm': k_norm,
            'sb_w_q': sb_w_q, 'sb_q_norm': sb_q_norm, 'sb_logit_bias': sb_logit_bias, 'sb_w_out': sb_w_out}


def reference(x_prompt, x_sample, state_gdn, state_conv, cache_k, cache_v, page_table,
              ffn1_norm, ffn1_w_gu, ffn1_w_down, ffn2_norm, ffn2_w_gu, ffn2_w_down, mix_norm,
              gdn_w_in, gdn_conv_w, gdn_a_log, gdn_dt_bias, gdn_o_norm, gdn_w_out,
              kv_norm, w_kv, k_norm, sb_w_q, sb_q_norm, sb_logit_bias, sb_w_out):

    def run_trunk(x, conv_bufs, gdn_states, past_k_segs, past_v_segs):
        past_len = sum(s.shape[1] for s in past_k_segs)
        new_conv, new_state = [], []
        k_segs = v_segs = k_new = v_new = None
        for layer in range(DEPTH):
            x = x + 0.5 * swiglu(rms_norm(x, ffn1_norm[layer]), ffn1_w_gu[layer], ffn1_w_down[layer])
            h = rms_norm(x, mix_norm[layer])
            if layer < N_A_LAYERS:
                out, cb, st = gdn_mixer(h, conv_bufs[layer], gdn_states[layer], gdn_w_in[layer], gdn_conv_w[layer],
                                        gdn_a_log[layer], gdn_dt_bias[layer], gdn_o_norm[layer], gdn_w_out[layer])
                new_conv.append(cb)
                new_state.append(st)
            else:
                j = layer - N_A_LAYERS
                out = sb_mixer(h, k_segs, v_segs, past_len, sb_w_q[j], sb_q_norm[j], sb_logit_bias[j], sb_w_out[j])
            x = x + out
            x = x + 0.5 * swiglu(rms_norm(x, ffn2_norm[layer]), ffn2_w_gu[layer], ffn2_w_down[layer])
            if layer == N_A_LAYERS - 1:
                k_new, v_new = shared_kv(x, kv_norm, w_kv, k_norm)
                k_segs = past_k_segs + (k_new,)
                v_segs = past_v_segs + (v_new,)
        return x, jnp.stack(new_conv), jnp.stack(new_state), k_new, v_new

    bp = x_prompt.shape[0]
    zero_conv = jnp.zeros((N_A_LAYERS, bp, CONV_W - 1, CONV_DIM), x_prompt.dtype)
    zero_state = jnp.zeros((N_A_LAYERS, bp, GDN_HEADS, GDN_DK, GDN_DV), jnp.float32)
    y_prompt, conv_p, gdn_p, k_p, v_p = run_trunk(x_prompt, zero_conv, zero_state, (), ())

    db, n_pages = page_table.shape
    past_k = cache_k[page_table].reshape(db, n_pages * cache_k.shape[1], SB_HEADS, SB_HEAD_DIM)
    past_v = cache_v[page_table].reshape(db, n_pages * cache_v.shape[1], SB_HEADS, SB_HEAD_DIM)
    y_sample, conv_s, gdn_s, k_s, v_s = run_trunk(x_sample, state_conv, state_gdn, (past_k,), (past_v,))

    return (y_prompt, y_sample, gdn_p.astype(x_prompt.dtype), conv_p, k_p, v_p,
            gdn_s.astype(x_sample.dtype), conv_s, k_s, v_s)
```

```python
import functools
import math

import jax
import jax.numpy as jnp
from jax import lax
from jax.experimental import pallas as pl
from jax.experimental.pallas import tpu as pltpu

F32 = jnp.float32
BF16 = jnp.bfloat16

EPS = 1e-6
HEAD_DIM = 128
GDN_CHUNK = 64
SUBLANES = 8
TOKEN_TILE = 512
ATT_TQ = 256
ATT_TK = 256
VMEM_LIMIT = 56 * 1024 * 1024
PREC_EXACT = lax.Precision.HIGHEST


def _params(*sem):
    return pltpu.CompilerParams(dimension_semantics=sem, vmem_limit_bytes=VMEM_LIMIT)


def _dot(a, b):
    return jnp.dot(a, b, preferred_element_type=F32)


def _dot_nt(a, b):
    return lax.dot_general(a, b, (((1,), (1,)), ((), ())), preferred_element_type=F32)


def _dot_tn(a, b):
    return lax.dot_general(a, b, (((0,), (0,)), ((), ())), preferred_element_type=F32)


def _dot_exact(a, b):
    return jnp.dot(a, b, preferred_element_type=F32, precision=PREC_EXACT)


def _rms(x, w):
    return x * lax.rsqrt(jnp.mean(x * x, axis=-1, keepdims=True) + EPS) * w


def _silu(x):
    return x * jax.nn.sigmoid(x)


def _softplus_neg_abs(x):
    return jnp.log1p(jnp.exp(-jnp.abs(x)))


def _row_tile(n_rows):
    return TOKEN_TILE if n_rows % TOKEN_TILE == 0 else n_rows


def _ffn_kernel(x_ref, nw_ref, wgu_ref, wd_ref, o_ref, *, d_ff, n_split):
    x = x_ref[...]
    h = _rms(x, nw_ref[...]).astype(BF16)
    fc = d_ff // n_split
    acc = jnp.zeros_like(x)
    for c in range(n_split):
        g = _dot(h, wgu_ref[:, c * fc:(c + 1) * fc])
        u = _dot(h, wgu_ref[:, d_ff + c * fc:d_ff + (c + 1) * fc])
        a = (_silu(g) * u).astype(BF16)
        acc = acc + _dot(a, wd_ref[c * fc:(c + 1) * fc, :])
    o_ref[...] = x + 0.5 * acc


def _ffn(x, norm_w, w_gu, w_down):
    t, d = x.shape
    d_ff = w_down.shape[0]
    tm = _row_tile(t)
    n_split = 2 if (d_ff // 2) % 128 == 0 else 1
    const = lambda i: (0, 0)
    return pl.pallas_call(
        functools.partial(_ffn_kernel, d_ff=d_ff, n_split=n_split),
        grid=(t // tm,),
        in_specs=[
            pl.BlockSpec((tm, d), lambda i: (i, 0)),
            pl.BlockSpec((1, d), const),
            pl.BlockSpec((d, 2 * d_ff), const, pipeline_mode=pl.Buffered(1)),
            pl.BlockSpec((d_ff, d), const, pipeline_mode=pl.Buffered(1)),
        ],
        out_specs=pl.BlockSpec((tm, d), lambda i: (i, 0)),
        out_shape=jax.ShapeDtypeStruct((t, d), F32),
        compiler_params=_params("parallel"),
        name="ffn",
    )(x, norm_w.reshape(1, d), w_gu, w_down)


def _normproj_kernel(x_ref, nw_ref, w_ref, o_ref):
    h = _rms(x_ref[...], nw_ref[...]).astype(BF16)
    o_ref[...] = _dot(h, w_ref[...])


def _normproj(x, norm_w, w):
    t, d = x.shape
    n = w.shape[1]
    tm = _row_tile(t)
    const = lambda i: (0, 0)
    return pl.pallas_call(
        _normproj_kernel,
        grid=(t // tm,),
        in_specs=[
            pl.BlockSpec((tm, d), lambda i: (i, 0)),
            pl.BlockSpec((1, d), const),
            pl.BlockSpec((d, n), const, pipeline_mode=pl.Buffered(1)),
        ],
        out_specs=pl.BlockSpec((tm, n), lambda i: (i, 0)),
        out_shape=jax.ShapeDtypeStruct((t, n), F32),
        compiler_params=_params("parallel"),
        name="gdn_in_proj",
    )(x, norm_w.reshape(1, d), w)


def _head_rms(y, w, h):
    seg = y[:, h * HEAD_DIM:(h + 1) * HEAD_DIM]
    return seg * lax.rsqrt(jnp.mean(seg * seg, axis=-1, keepdims=True) + EPS) * w


def _kv_kernel(x_ref, nw_ref, w_ref, kn_ref, k_ref, v_ref, kb_ref, vb_ref, *, n_heads):
    hid = _rms(x_ref[...], nw_ref[...]).astype(BF16)
    kv = _dot(hid, w_ref[...])
    d = n_heads * HEAD_DIM
    for h in range(n_heads):
        sl = slice(h * HEAD_DIM, (h + 1) * HEAD_DIM)
        k = _head_rms(kv, kn_ref[...], h)
        k_ref[:, sl] = k
        kb_ref[:, sl] = k.astype(BF16)
    v = kv[:, d:]
    v_ref[...] = v
    vb_ref[...] = v.astype(BF16)


def _shared_kv(x, kv_norm, w_kv, k_norm):
    t, d = x.shape
    n_heads = d // HEAD_DIM
    tm = _row_tile(t)
    const = lambda i: (0, 0)
    row = pl.BlockSpec((tm, d), lambda i: (i, 0))
    return pl.pallas_call(
        functools.partial(_kv_kernel, n_heads=n_heads),
        grid=(t // tm,),
        in_specs=[
            row,
            pl.BlockSpec((1, d), const),
            pl.BlockSpec((d, 2 * d), const, pipeline_mode=pl.Buffered(1)),
            pl.BlockSpec((1, HEAD_DIM), const),
        ],
        out_specs=[row, row, row, row],
        out_shape=[jax.ShapeDtypeStruct((t, d), F32), jax.ShapeDtypeStruct((t, d), F32),
                   jax.ShapeDtypeStruct((t, d), BF16), jax.ShapeDtypeStruct((t, d), BF16)],
        compiler_params=_params("parallel"),
        name="shared_kv",
    )(x, kv_norm.reshape(1, d), w_kv, k_norm.reshape(1, HEAD_DIM))


def _q_kernel(x_ref, nw_ref, w_ref, qn_ref, q_ref, *, n_heads):
    hid = _rms(x_ref[...], nw_ref[...]).astype(BF16)
    q = _dot(hid, w_ref[...])
    for h in range(n_heads):
        q_ref[:, h * HEAD_DIM:(h + 1) * HEAD_DIM] = _head_rms(q, qn_ref[...], h).astype(BF16)


def _sb_q(x, norm_w, w_q, q_norm):
    t, d = x.shape
    n_heads = d // HEAD_DIM
    tm = _row_tile(t)
    const = lambda i: (0, 0)
    row = pl.BlockSpec((tm, d), lambda i: (i, 0))
    return pl.pallas_call(
        functools.partial(_q_kernel, n_heads=n_heads),
        grid=(t // tm,),
        in_specs=[
            row,
            pl.BlockSpec((1, d), const),
            pl.BlockSpec((d, d), const, pipeline_mode=pl.Buffered(1)),
            pl.BlockSpec((1, HEAD_DIM), const),
        ],
        out_specs=row,
        out_shape=jax.ShapeDtypeStruct((t, d), BF16),
        compiler_params=_params("parallel"),
        name="sb_q_proj",
    )(x, norm_w.reshape(1, d), w_q, q_norm.reshape(1, HEAD_DIM))


def _outproj_kernel(x_ref, a_ref, w_ref, o_ref):
    o_ref[...] = x_ref[...] + _dot(a_ref[...], w_ref[...])


def _outproj(x, a, w):
    t, d = x.shape
    tm = _row_tile(t)
    row = pl.BlockSpec((tm, d), lambda i: (i, 0))
    return pl.pallas_call(
        _outproj_kernel,
        grid=(t // tm,),
        in_specs=[row, row, pl.BlockSpec((d, d), lambda i: (0, 0), pipeline_mode=pl.Buffered(1))],
        out_specs=row,
        out_shape=jax.ShapeDtypeStruct((t, d), F32),
        compiler_params=_params("parallel"),
        name="sb_out_proj",
    )(x, a, w)


def _unit_lower_inverse(m, c):
    eye = (lax.broadcasted_iota(jnp.int32, (c, c), 0) == lax.broadcasted_iota(jnp.int32, (c, c), 1)).astype(F32)
    t = eye - m
    p = m
    for _ in range(int(math.log2(c)) - 1):
        p = _dot_exact(p, p)
        t = t + _dot_exact(t, p)
    return t


def _gdn_kernel(proj_ref, tail_ref, s0_ref, x_ref, cw_ref, alog_ref, dtb_ref, onorm_ref, wout_ref,
                xo_ref, sf_ref, state, xc, tail_s, og, *, n_heads, chunk, rows, valid_rows, n_blocks):
    qk_dim = n_heads * HEAD_DIM
    blk = pl.program_id(1)

    @pl.when(blk == 0)
    def _():
        state[...] = s0_ref[0]
        tail_s[...] = tail_ref[0]

    xc[0:SUBLANES, :] = tail_s[...]
    xc[SUBLANES:SUBLANES + rows, :] = proj_ref[:, 0:3 * qk_dim]
    tail_s[...] = xc[rows:rows + SUBLANES, :]

    c = chunk
    ri = lax.broadcasted_iota(jnp.int32, (c, c), 0)
    ci = lax.broadcasted_iota(jnp.int32, (c, c), 1)
    lower_incl = (ri >= ci)
    lower_strict = (ri > ci)
    cum_mat = lower_incl.astype(F32)
    neg_a = -jnp.exp(alog_ref[...])
    dtb = dtb_ref[...]
    onorm = onorm_ref[...]
    conv_taps = cw_ref.shape[0]

    def conv_seg(r0, col0):
        win = xc[pl.ds(r0, c + SUBLANES), col0:col0 + HEAD_DIM]
        acc = None
        for w in range(conv_taps):
            shift = conv_taps - 1 - w
            xw = win if shift == 0 else pltpu.roll(win, shift, axis=0)
            term = xw[SUBLANES:, :] * cw_ref[w:w + 1, col0:col0 + HEAD_DIM]
            acc = term if acc is None else acc + term
        return _silu(acc)

    def l2n(v):
        return v * lax.rsqrt(jnp.sum(v * v, axis=-1, keepdims=True) + EPS)

    def chunk_body(ch, carry):
        r0 = pl.multiple_of(ch * c, c)
        gate = proj_ref[pl.ds(r0, c), 4 * qk_dim:4 * qk_dim + HEAD_DIM]
        beta_all = jax.nn.sigmoid(gate)
        gpre = gate + dtb
        g_all = neg_a * (jnp.maximum(gpre, 0.0) + _softplus_neg_abs(gpre))
        if valid_rows < c:
            live = lax.broadcasted_iota(jnp.int32, (c, HEAD_DIM), 0) < valid_rows
            beta_all = jnp.where(live, beta_all, 0.0)
            g_all = jnp.where(live, g_all, 0.0)
        gc_all = _dot_exact(cum_mat, g_all)
        gc_all_t = gc_all.T
        for h in range(n_heads):
            q = l2n(conv_seg(r0, h * HEAD_DIM)) * (HEAD_DIM ** -0.5)
            k = l2n(conv_seg(r0, qk_dim + h * HEAD_DIM))
            v = conv_seg(r0, 2 * qk_dim + h * HEAD_DIM)
            beta = beta_all[:, h:h + 1]
            gcol = gc_all[:, n_heads + h:n_heads + h + 1]
            grow = gc_all_t[n_heads + h:n_heads + h + 1, :]
            decay = jnp.exp(jnp.where(lower_incl, gcol - grow, -jnp.inf))
            kb = k.astype(BF16)
            kk = _dot_nt(kb, kb)
            m = jnp.where(lower_strict, beta * kk * decay, 0.0)
            t_inv = _unit_lower_inverse(m, c)
            gam = jnp.exp(gcol)
            w_mat = _dot_exact(t_inv, (beta * gam) * k)
            u_mat = _dot_exact(t_inv, beta * v)
            qk = _dot_nt(q.astype(BF16), kb) * decay
            g_last = gcol[c - 1:c, :]
            k_tail = k * jnp.exp(g_last - gcol)
            s = state[h]
            sb = s.astype(BF16)
            u = u_mat - _dot(w_mat.astype(BF16), sb)
            ub = u.astype(BF16)
            o = _dot((q * gam).astype(BF16), sb) + _dot(qk.astype(BF16), ub)
            state[h] = jnp.exp(g_last) * s + _dot_tn(k_tail.astype(BF16), ub)
            z = proj_ref[pl.ds(r0, c), 3 * qk_dim + h * HEAD_DIM:3 * qk_dim + (h + 1) * HEAD_DIM]
            og[pl.ds(r0, c), h * HEAD_DIM:(h + 1) * HEAD_DIM] = _rms(o, onorm) * _silu(z)
        return carry

    lax.fori_loop(0, rows // c, chunk_body, 0)
    xo_ref[...] = x_ref[...] + _dot(og[...].astype(BF16), wout_ref[...])

    @pl.when(blk == n_blocks - 1)
    def _():
        sf_ref[0] = state[...]


def _gdn_layer(x, proj, tail, s0, conv_w, a_log, dt_bias, o_norm, w_out, *, n_seq, seq_rows, chunk, valid_rows):
    t, d = x.shape
    n_heads = d // HEAD_DIM
    np_cols = proj.shape[1]
    rows = min(seq_rows, TOKEN_TILE)
    n_blocks = seq_rows // rows
    lane_pad = HEAD_DIM - 2 * n_heads
    alog_l = jnp.pad(a_log, (n_heads, lane_pad)).reshape(1, HEAD_DIM)
    dtb_l = jnp.pad(dt_bias, (n_heads, lane_pad)).reshape(1, HEAD_DIM)
    const2 = lambda b, r: (0, 0)
    row_map = lambda b, r: (b * n_blocks + r, 0)
    seq_map4 = lambda b, r: (b, 0, 0, 0)
    return pl.pallas_call(
        functools.partial(_gdn_kernel, n_heads=n_heads, chunk=chunk, rows=rows, valid_rows=valid_rows,
                          n_blocks=n_blocks),
        grid=(n_seq, n_blocks),
        in_specs=[
            pl.BlockSpec((rows, np_cols), row_map),
            pl.BlockSpec((1, SUBLANES, 3 * d), lambda b, r: (b, 0, 0)),
            pl.BlockSpec((1, n_heads, HEAD_DIM, HEAD_DIM), seq_map4),
            pl.BlockSpec((rows, d), row_map),
            pl.BlockSpec(conv_w.shape, const2),
            pl.BlockSpec((1, HEAD_DIM), const2),
            pl.BlockSpec((1, HEAD_DIM), const2),
            pl.BlockSpec((1, HEAD_DIM), const2),
            pl.BlockSpec((d, d), const2, pipeline_mode=pl.Buffered(1)),
        ],
        out_specs=[
            pl.BlockSpec((rows, d), row_map),
            pl.BlockSpec((1, n_heads, HEAD_DIM, HEAD_DIM), seq_map4),
        ],
        out_shape=[jax.ShapeDtypeStruct((t, d), F32),
                   jax.ShapeDtypeStruct((n_seq, n_heads, HEAD_DIM, HEAD_DIM), F32)],
        scratch_shapes=[
            pltpu.VMEM((n_heads, HEAD_DIM, HEAD_DIM), F32),
            pltpu.VMEM((rows + SUBLANES, 3 * d), F32),
            pltpu.VMEM((SUBLANES, 3 * d), F32),
            pltpu.VMEM((rows, d), F32),
        ],
        compiler_params=_params("arbitrary", "arbitrary"),
        name="gdn_mixer",
    )(proj, tail, s0, x, conv_w, alog_l, dtb_l, o_norm.reshape(1, HEAD_DIM), w_out)


def _sb_terms(z):
    sp = _softplus_neg_abs(z)
    return jnp.minimum(z, 0.0) - sp, -jnp.maximum(z, 0.0) - sp


def _split_bf16(x):
    hi = x.astype(BF16)
    lo = (x - hi.astype(F32)).astype(BF16)
    return hi, lo


def _sb_prompt_kernel(bias_ref, q_ref, k_ref, v_ref, suf_ref, o_ref, *, tq, tk, scale):
    head = pl.program_id(1)
    qi = pl.program_id(2)
    bias = bias_ref[head]
    q = q_ref[...]
    suf = suf_ref[...]

    def block(j, carry, diagonal):
        run, acc = carry
        k0 = pl.multiple_of(j * tk, tk)
        kb = k_ref[pl.ds(k0, tk), :]
        vb = v_ref[pl.ds(k0, tk), :]
        z = _dot_nt(q, kb) * scale + bias
        ls_pos, ls_neg = _sb_terms(z)
        if diagonal:
            earlier = (lax.broadcasted_iota(jnp.int32, (tq, tk), 1) < lax.broadcasted_iota(jnp.int32, (tq, tk), 0))
            ls_neg = jnp.where(earlier, ls_neg, 0.0)
        hi, lo = _split_bf16(ls_neg)
        within = _dot(hi, suf) + _dot(lo, suf)
        logw = ls_pos + (run + within)
        if diagonal:
            logw = jnp.where(earlier, logw, -jnp.inf)
        w = jnp.exp(logw).astype(BF16)
        acc = acc + _dot(w, vb)
        run = run + jnp.sum(ls_neg, axis=-1, keepdims=True)
        return run, acc

    carry = (jnp.zeros((tq, 1), F32), jnp.zeros((tq, HEAD_DIM), F32))
    carry = block(qi, carry, True)
    carry = lax.fori_loop(0, qi, lambda i, cr: block(qi - 1 - i, cr, False), carry)
    o_ref[...] = carry[1].astype(o_ref.dtype)


def _sb_prompt_attention(q, k, v, logit_bias, *, n_seq, seq_len):
    t, d = q.shape
    n_heads = d // HEAD_DIM
    tq = min(ATT_TQ, seq_len)
    tk = tq
    n_qb = seq_len // tq
    suf = (jnp.arange(tk)[:, None] > jnp.arange(tk)[None, :]).astype(BF16)
    return pl.pallas_call(
        functools.partial(_sb_prompt_kernel, tq=tq, tk=tk, scale=HEAD_DIM ** -0.5),
        grid=(n_seq, n_heads, n_qb),
        in_specs=[
            pl.BlockSpec(memory_space=pltpu.SMEM),
            pl.BlockSpec((tq, HEAD_DIM), lambda b, h, i: (b * n_qb + i, h)),
            pl.BlockSpec((seq_len, HEAD_DIM), lambda b, h, i: (b, h)),
            pl.BlockSpec((seq_len, HEAD_DIM), lambda b, h, i: (b, h)),
            pl.BlockSpec((tk, tk), lambda b, h, i: (0, 0)),
        ],
        out_specs=pl.BlockSpec((tq, HEAD_DIM), lambda b, h, i: (b * n_qb + i, h)),
        out_shape=jax.ShapeDtypeStruct((t, d), BF16),
        compiler_params=_params("parallel", "parallel", "arbitrary"),
        name="sb_prompt_attention",
    )(logit_bias.astype(F32), q, k, v, suf)


def _sb_sample_kernel(pt_ref, q_ref, kn_ref, vn_ref, kc_ref, vc_ref, bias_ref, o_ref, run, acc,
                      *, n_heads, n_q, n_new, page, n_pages, scale):
    del pt_ref
    step = pl.program_id(1)
    qrows = n_q * n_heads
    q = q_ref[0]
    sub = lax.broadcasted_iota(jnp.int32, (n_heads, 1), 0)

    def attend(kflat, vflat, n_keys, new_keys):
        lanes = n_keys * n_heads
        lane = lax.broadcasted_iota(jnp.int32, (1, lanes), 1)
        head_sel = (lane & (n_heads - 1)) == sub
        z_all = _dot_nt(q, kflat)
        rows = []
        for i in range(n_q):
            zi = jnp.where(head_sel, z_all[i * n_heads:(i + 1) * n_heads, :], 0.0)
            rows.append(jnp.sum(zi, axis=0, keepdims=True))
        z = jnp.concatenate(rows + [jnp.zeros((SUBLANES - n_q, lanes), F32)], axis=0)
        reps = lanes // HEAD_DIM
        z = z * scale + jnp.concatenate([bias_ref[...]] * reps, axis=1)
        ls_pos, ls_neg = _sb_terms(z)
        qidx = lax.broadcasted_iota(jnp.int32, (SUBLANES, lanes), 0)
        live = qidx < n_q
        if new_keys:
            key = lax.broadcasted_iota(jnp.int32, (SUBLANES, lanes), 1) // n_heads
            live = live & (key < qidx) & (key < n_new)
        ls_neg = jnp.where(live, ls_neg, 0.0)
        lane_full = lax.broadcasted_iota(jnp.int32, (SUBLANES, lanes), 1)
        incl = ls_neg
        tot = ls_neg
        s = n_heads
        while s < lanes:
            shifted = pltpu.roll(incl, lanes - s, axis=1)
            incl = incl + jnp.where(lane_full + s < lanes, shifted, 0.0)
            tot = tot + pltpu.roll(tot, s, axis=1)
            s *= 2
        run_full = jnp.concatenate([run[...]] * reps, axis=1)
        logw = ls_pos + run_full + (incl - ls_neg)
        w = jnp.exp(jnp.where(live, logw, -jnp.inf))
        w_rows = [jnp.where(head_sel, w[i:i + 1, :], 0.0) for i in range(n_q)]
        w_exp = jnp.concatenate(w_rows, axis=0).astype(BF16)
        acc[...] += _dot(w_exp, vflat)
        run[...] += tot[:, 0:HEAD_DIM]

    @pl.when(step == 0)
    def _():
        run[...] = jnp.zeros_like(run)
        acc[...] = jnp.zeros_like(acc)
        pad_keys = HEAD_DIM // n_heads
        zpad = jnp.zeros(((pad_keys - n_new) * n_heads, HEAD_DIM), BF16)
        attend(jnp.concatenate([kn_ref[0], zpad], axis=0), jnp.concatenate([vn_ref[0], zpad], axis=0),
               pad_keys, True)

    attend(kc_ref[0].astype(BF16), vc_ref[0].astype(BF16), page, False)

    @pl.when(step == n_pages - 1)
    def _():
        o_ref[0] = acc[...].astype(o_ref.dtype)


def _sb_sample_attention(q, k_new, v_new, cache_k, cache_v, page_table, logit_bias, *, n_seq, n_q):
    t, d = q.shape
    n_heads = d // HEAD_DIM
    n_phys, page = cache_k.shape[0], cache_k.shape[1]
    n_pages = page_table.shape[1]
    qrows = n_q * n_heads
    q3 = q.reshape(n_seq, qrows, HEAD_DIM)
    kn3 = k_new.reshape(n_seq, qrows, HEAD_DIM)
    vn3 = v_new.reshape(n_seq, qrows, HEAD_DIM)
    kc = cache_k.reshape(n_phys, page * n_heads, HEAD_DIM)
    vc = cache_v.reshape(n_phys, page * n_heads, HEAD_DIM)
    bias_l = jnp.tile(logit_bias.astype(F32), HEAD_DIM // n_heads).reshape(1, HEAD_DIM)
    seq3 = lambda b, j, pt: (b, 0, 0)
    page_map = lambda b, j, pt: (pt[b, n_pages - 1 - j], 0, 0)
    grid_spec = pltpu.PrefetchScalarGridSpec(
        num_scalar_prefetch=1,
        grid=(n_seq, n_pages),
        in_specs=[
            pl.BlockSpec((1, qrows, HEAD_DIM), seq3),
            pl.BlockSpec((1, qrows, HEAD_DIM), seq3),
            pl.BlockSpec((1, qrows, HEAD_DIM), seq3),
            pl.BlockSpec((1, page * n_heads, HEAD_DIM), page_map),
            pl.BlockSpec((1, page * n_heads, HEAD_DIM), page_map),
            pl.BlockSpec((1, HEAD_DIM), lambda b, j, pt: (0, 0)),
        ],
        out_specs=pl.BlockSpec((1, qrows, HEAD_DIM), seq3),
        scratch_shapes=[pltpu.VMEM((SUBLANES, HEAD_DIM), F32), pltpu.VMEM((qrows, HEAD_DIM), F32)],
    )
    out = pl.pallas_call(
        functools.partial(_sb_sample_kernel, n_heads=n_heads, n_q=n_q, n_new=n_q, page=page, n_pages=n_pages,
                          scale=HEAD_DIM ** -0.5),
        grid_spec=grid_spec,
        out_shape=jax.ShapeDtypeStruct((n_seq, qrows, HEAD_DIM), BF16),
        compiler_params=_params("parallel", "arbitrary"),
        name="sb_sample_attention",
    )(page_table, q3, kn3, vn3, kc, vc, bias_l)
    return out.reshape(t, d)


def _trunk(x, tail, s0, past, w, *, n_seq, seq_len):
    d = x.shape[1]
    qkv_cols = 3 * d
    x = _ffn(x, w["ffn1_norm"][0], w["ffn1_w_gu"][0], w["ffn1_w_down"][0])
    proj = _normproj(x, w["mix_norm"][0], w["gdn_w_in"])
    seq_rows = -(-seq_len // SUBLANES) * SUBLANES
    chunk = min(GDN_CHUNK, seq_rows)
    pad = lambda a: jnp.pad(a.reshape(n_seq, seq_len, -1), ((0, 0), (0, seq_rows - seq_len), (0, 0))
                            ).reshape(n_seq * seq_rows, -1)
    xg, pg = (x, proj) if seq_rows == seq_len else (pad(x), pad(proj))
    xg, state = _gdn_layer(xg, pg, tail, s0, w["gdn_conv_w"], w["gdn_a_log"], w["gdn_dt_bias"], w["gdn_o_norm"],
                           w["gdn_w_out"], n_seq=n_seq, seq_rows=seq_rows, chunk=chunk,
                           valid_rows=min(seq_len, chunk))
    x = xg if seq_rows == seq_len else xg.reshape(n_seq, seq_rows, d)[:, :seq_len].reshape(n_seq * seq_len, d)
    x = _ffn(x, w["ffn2_norm"][0], w["ffn2_w_gu"][0], w["ffn2_w_down"][0])
    k, v, kb, vb = _shared_kv(x, w["kv_norm"], w["w_kv"], w["k_norm"])
    x = _ffn(x, w["ffn1_norm"][1], w["ffn1_w_gu"][1], w["ffn1_w_down"][1])
    q = _sb_q(x, w["mix_norm"][1], w["sb_w_q"], w["sb_q_norm"])
    if past is None:
        att = _sb_prompt_attention(q, kb, vb, w["sb_logit_bias"], n_seq=n_seq, seq_len=seq_len)
    else:
        att = _sb_sample_attention(q, kb, vb, past[0], past[1], past[2], w["sb_logit_bias"], n_seq=n_seq, n_q=seq_len)
    x = _outproj(x, att, w["sb_w_out"])
    x = _ffn(x, w["ffn2_norm"][1], w["ffn2_w_gu"][1], w["ffn2_w_down"][1])
    raw_qkv = proj[:, :qkv_cols].reshape(n_seq, seq_len, qkv_cols)
    return x, raw_qkv, state, k, v


def kernel(x_prompt, x_sample, state_gdn, state_conv, cache_k, cache_v, page_table, ffn1_norm, ffn1_w_gu,
           ffn1_w_down, ffn2_norm, ffn2_w_gu, ffn2_w_down, mix_norm, gdn_w_in, gdn_conv_w, gdn_a_log, gdn_dt_bias,
           gdn_o_norm, gdn_w_out, kv_norm, w_kv, k_norm, sb_w_q, sb_q_norm, sb_logit_bias, sb_w_out):
    bp, seq, d = x_prompt.shape
    bs, dec = x_sample.shape[0], x_sample.shape[1]
    n_heads = d // HEAD_DIM
    hist = state_conv.shape[2]
    assert gdn_w_in.shape[0] == 1 and sb_w_q.shape[0] == 1, "one layer of each mixer kind"
    assert dec <= SUBLANES and hist < SUBLANES and dec >= hist

    in_cols = gdn_w_in.shape[2]
    np_cols = -(-in_cols // HEAD_DIM) * HEAD_DIM
    w = {
        "ffn1_norm": ffn1_norm, "ffn2_norm": ffn2_norm, "mix_norm": mix_norm,
        "ffn1_w_gu": ffn1_w_gu.astype(BF16), "ffn1_w_down": ffn1_w_down.astype(BF16),
        "ffn2_w_gu": ffn2_w_gu.astype(BF16), "ffn2_w_down": ffn2_w_down.astype(BF16),
        "gdn_w_in": jnp.pad(gdn_w_in[0], ((0, 0), (0, np_cols - in_cols))).astype(BF16),
        "gdn_conv_w": gdn_conv_w[0], "gdn_a_log": gdn_a_log[0], "gdn_dt_bias": gdn_dt_bias[0],
        "gdn_o_norm": gdn_o_norm[0], "gdn_w_out": gdn_w_out[0].astype(BF16),
        "kv_norm": kv_norm, "w_kv": w_kv.astype(BF16), "k_norm": k_norm,
        "sb_w_q": sb_w_q[0].astype(BF16), "sb_q_norm": sb_q_norm[0], "sb_logit_bias": sb_logit_bias[0],
        "sb_w_out": sb_w_out[0].astype(BF16),
    }

    zero_tail = jnp.zeros((bp, SUBLANES, 3 * d), F32)
    zero_state = jnp.zeros((bp, n_heads, HEAD_DIM, HEAD_DIM), F32)
    y_p, qkv_p, st_p, k_p, v_p = _trunk(x_prompt.reshape(bp * seq, d), zero_tail, zero_state, None, w,
                                        n_seq=bp, seq_len=seq)

    tail_s = jnp.pad(state_conv[0], ((0, 0), (SUBLANES - hist, 0), (0, 0)))
    y_s, qkv_s, st_s, k_s, v_s = _trunk(x_sample.reshape(bs * dec, d), tail_s, state_gdn[0],
                                        (cache_k, cache_v, page_table), w, n_seq=bs, seq_len=dec)

    conv_p = qkv_p[:, seq - hist:seq, :][None]
    conv_s = qkv_s[:, dec - hist:dec, :][None]
    return (y_p.reshape(bp, seq, d), y_s.reshape(bs, dec, d), st_p[None].astype(x_prompt.dtype), conv_p,
            k_p.reshape(bp, seq, n_heads, HEAD_DIM), v_p.reshape(bp, seq, n_heads, HEAD_DIM),
            st_s[None].astype(x_sample.dtype), conv_s,
            k_s.reshape(bs, dec, n_heads, HEAD_DIM), v_s.reshape(bs, dec, n_heads, HEAD_DIM))
```

```python
import functools
import math

import jax
import jax.numpy as jnp
from jax import lax
from jax.experimental import pallas as pl
from jax.experimental.pallas import tpu as pltpu

F32 = jnp.float32
BF16 = jnp.bfloat16

EPS = 1e-6
HEAD_DIM = 128
GDN_CHUNK = 64
SUBLANES = 8
TOKEN_TILE = 512
ATT_TILE = 256
SAMPLE_PAGES_PER_STEP = 8
SAMPLE_SEQS_PER_STEP = 4
VMEM_LIMIT = 56 * 1024 * 1024
PREC_EXACT = lax.Precision.HIGHEST
LOG2E = math.log2(math.e)
Q_SCALE = HEAD_DIM ** -0.5 * LOG2E


def _params(*sem):
    return pltpu.CompilerParams(dimension_semantics=sem, vmem_limit_bytes=VMEM_LIMIT)


def _dot(a, b):
    return jnp.dot(a, b, preferred_element_type=F32)


def _dot_nt(a, b):
    return lax.dot_general(a, b, (((1,), (1,)), ((), ())), preferred_element_type=F32)


def _dot_exact(a, b):
    return jnp.dot(a, b, preferred_element_type=F32, precision=PREC_EXACT)


def _rms(x, w):
    return x * lax.rsqrt(jnp.mean(x * x, axis=-1, keepdims=True) + EPS) * w


def _silu(x):
    return x * jax.nn.sigmoid(x)


def _softplus_neg_abs(x):
    return jnp.log1p(jnp.exp(-jnp.abs(x)))


def _row_tile(n_rows):
    return TOKEN_TILE if n_rows % TOKEN_TILE == 0 else n_rows


def _ffn_kernel(x_ref, nw_ref, wgu_ref, wd_ref, o_ref, *, d_ff, n_split):
    x = x_ref[...]
    h = _rms(x, nw_ref[...]).astype(BF16)
    fc = d_ff // n_split
    acc = jnp.zeros_like(x)
    for c in range(n_split):
        g = _dot(h, wgu_ref[:, c * fc:(c + 1) * fc])
        u = _dot(h, wgu_ref[:, d_ff + c * fc:d_ff + (c + 1) * fc])
        a = (_silu(g) * u).astype(BF16)
        acc = acc + _dot(a, wd_ref[c * fc:(c + 1) * fc, :])
    o_ref[...] = x + 0.5 * acc


def _ffn(x, norm_w, w_gu, w_down):
    t, d = x.shape
    d_ff = w_down.shape[0]
    tm = _row_tile(t)
    n_split = 2 if (d_ff // 2) % 128 == 0 else 1
    const = lambda i: (0, 0)
    return pl.pallas_call(
        functools.partial(_ffn_kernel, d_ff=d_ff, n_split=n_split),
        grid=(t // tm,),
        in_specs=[
            pl.BlockSpec((tm, d), lambda i: (i, 0)),
            pl.BlockSpec((1, d), const),
            pl.BlockSpec((d, 2 * d_ff), const, pipeline_mode=pl.Buffered(1)),
            pl.BlockSpec((d_ff, d), const, pipeline_mode=pl.Buffered(1)),
        ],
        out_specs=pl.BlockSpec((tm, d), lambda i: (i, 0)),
        out_shape=jax.ShapeDtypeStruct((t, d), F32),
        compiler_params=_params("parallel"),
        name="ffn",
    )(x, norm_w.reshape(1, d), w_gu, w_down)


def _normproj_kernel(x_ref, nw_ref, w_ref, o_ref):
    h = _rms(x_ref[...], nw_ref[...]).astype(BF16)
    o_ref[...] = _dot(h, w_ref[...])


def _normproj(x, norm_w, w):
    t, d = x.shape
    n = w.shape[1]
    tm = _row_tile(t)
    const = lambda i: (0, 0)
    return pl.pallas_call(
        _normproj_kernel,
        grid=(t // tm,),
        in_specs=[
            pl.BlockSpec((tm, d), lambda i: (i, 0)),
            pl.BlockSpec((1, d), const),
            pl.BlockSpec((d, n), const, pipeline_mode=pl.Buffered(1)),
        ],
        out_specs=pl.BlockSpec((tm, n), lambda i: (i, 0)),
        out_shape=jax.ShapeDtypeStruct((t, n), F32),
        compiler_params=_params("parallel"),
        name="gdn_in_proj",
    )(x, norm_w.reshape(1, d), w)


def _head_rms(y, w, h):
    seg = y[:, h * HEAD_DIM:(h + 1) * HEAD_DIM]
    return seg * lax.rsqrt(jnp.mean(seg * seg, axis=-1, keepdims=True) + EPS) * w


def _kv_kernel(x_ref, nw_ref, w_ref, kn_ref, k_ref, v_ref, kb_ref, vb_ref, *, n_heads):
    hid = _rms(x_ref[...], nw_ref[...]).astype(BF16)
    kv = _dot(hid, w_ref[...])
    d = n_heads * HEAD_DIM
    for h in range(n_heads):
        sl = slice(h * HEAD_DIM, (h + 1) * HEAD_DIM)
        k = _head_rms(kv, kn_ref[...], h)
        k_ref[:, sl] = k
        kb_ref[:, sl] = k.astype(BF16)
    v = kv[:, d:]
    v_ref[...] = v
    vb_ref[...] = v.astype(BF16)


def _shared_kv(x, kv_norm, w_kv, k_norm):
    t, d = x.shape
    n_heads = d // HEAD_DIM
    tm = _row_tile(t)
    const = lambda i: (0, 0)
    row = pl.BlockSpec((tm, d), lambda i: (i, 0))
    return pl.pallas_call(
        functools.partial(_kv_kernel, n_heads=n_heads),
        grid=(t // tm,),
        in_specs=[
            row,
            pl.BlockSpec((1, d), const),
            pl.BlockSpec((d, 2 * d), const, pipeline_mode=pl.Buffered(1)),
            pl.BlockSpec((1, HEAD_DIM), const),
        ],
        out_specs=[row, row, row, row],
        out_shape=[jax.ShapeDtypeStruct((t, d), F32), jax.ShapeDtypeStruct((t, d), F32),
                   jax.ShapeDtypeStruct((t, d), BF16), jax.ShapeDtypeStruct((t, d), BF16)],
        compiler_params=_params("parallel"),
        name="shared_kv",
    )(x, kv_norm.reshape(1, d), w_kv, k_norm.reshape(1, HEAD_DIM))


def _q_kernel(x_ref, nw_ref, w_ref, qn_ref, q_ref, *, n_heads):
    hid = _rms(x_ref[...], nw_ref[...]).astype(BF16)
    q = _dot(hid, w_ref[...])
    for h in range(n_heads):
        q_ref[:, h * HEAD_DIM:(h + 1) * HEAD_DIM] = (_head_rms(q, qn_ref[...], h) * Q_SCALE).astype(BF16)


def _sb_q(x, norm_w, w_q, q_norm):
    t, d = x.shape
    n_heads = d // HEAD_DIM
    tm = _row_tile(t)
    const = lambda i: (0, 0)
    row = pl.BlockSpec((tm, d), lambda i: (i, 0))
    return pl.pallas_call(
        functools.partial(_q_kernel, n_heads=n_heads),
        grid=(t // tm,),
        in_specs=[
            row,
            pl.BlockSpec((1, d), const),
            pl.BlockSpec((d, d), const, pipeline_mode=pl.Buffered(1)),
            pl.BlockSpec((1, HEAD_DIM), const),
        ],
        out_specs=row,
        out_shape=jax.ShapeDtypeStruct((t, d), BF16),
        compiler_params=_params("parallel"),
        name="sb_q_proj",
    )(x, norm_w.reshape(1, d), w_q, q_norm.reshape(1, HEAD_DIM))


def _outproj_kernel(x_ref, a_ref, w_ref, o_ref):
    o_ref[...] = x_ref[...] + _dot(a_ref[...], w_ref[...])


def _outproj(x, a, w):
    t, d = x.shape
    tm = _row_tile(t)
    row = pl.BlockSpec((tm, d), lambda i: (i, 0))
    return pl.pallas_call(
        _outproj_kernel,
        grid=(t // tm,),
        in_specs=[row, row, pl.BlockSpec((d, d), lambda i: (0, 0), pipeline_mode=pl.Buffered(1))],
        out_specs=row,
        out_shape=jax.ShapeDtypeStruct((t, d), F32),
        compiler_params=_params("parallel"),
        name="sb_out_proj",
    )(x, a, w)


def _unit_lower_inverses(ms, c, merge_masks):
    n_blk = c // SUBLANES
    col_id = lax.broadcasted_iota(jnp.int32, (SUBLANES, c), 1)
    row_id = lax.broadcasted_iota(jnp.int32, (SUBLANES, c), 0)
    m_blk = [[m[b * SUBLANES:(b + 1) * SUBLANES, :] for b in range(n_blk)] for m in ms]
    t_blk = [[(col_id == row_id + b * SUBLANES).astype(F32) for b in range(n_blk)] for _ in ms]
    for j in range(SUBLANES - 1):
        for i in range(len(ms)):
            for b in range(n_blk):
                col = m_blk[i][b][:, b * SUBLANES + j:b * SUBLANES + j + 1]
                t_blk[i][b] = t_blk[i][b] - col * t_blk[i][b][j:j + 1, :]
    ts = [jnp.concatenate(t, axis=0) if n_blk > 1 else t[0] for t in t_blk]
    for off_diag in merge_masks:
        cts = [_dot(jnp.where(off_diag, m, 0.0).astype(BF16), t.astype(BF16)) for m, t in zip(ms, ts)]
        ts = [t - _dot(t.astype(BF16), ct.astype(BF16)) for t, ct in zip(ts, cts)]
    return ts


def _merge_masks(c):
    ri = lax.broadcasted_iota(jnp.int32, (c, c), 0)
    ci = lax.broadcasted_iota(jnp.int32, (c, c), 1)
    masks = []
    shift = int(math.log2(SUBLANES))
    while (1 << shift) < c:
        same_pair = lax.shift_right_logical(ri, shift + 1) == lax.shift_right_logical(ci, shift + 1)
        same_blk = lax.shift_right_logical(ri, shift) == lax.shift_right_logical(ci, shift)
        masks.append(same_pair & jnp.logical_not(same_blk))
        shift += 1
    return masks


def _gdn_kernel(proj_ref, tail_ref, s0_ref, x_ref, cw_ref, alog_ref, dtb_ref, onorm_ref, wout_ref,
                xo_ref, sf_ref, state, xc, tail_s, og, *, n_heads, chunk, rows, valid_rows, n_blocks, n_seqs):
    qk_dim = n_heads * HEAD_DIM
    blk = pl.program_id(1)

    @pl.when(blk == 0)
    def _():
        state[...] = s0_ref[...]
        tail_s[...] = tail_ref[...]

    for sq in range(n_seqs):
        xc[sq, 0:SUBLANES, :] = tail_s[sq]
        xc[sq, SUBLANES:SUBLANES + rows, :] = proj_ref[sq * rows:(sq + 1) * rows, 0:3 * qk_dim]
        tail_s[sq] = xc[sq, rows:rows + SUBLANES, :]

    c = chunk
    ri = lax.broadcasted_iota(jnp.int32, (c, c), 0)
    ci = lax.broadcasted_iota(jnp.int32, (c, c), 1)
    lower_incl = (ri >= ci)
    lower_strict = (ri > ci)
    cum_mat = lower_incl.astype(F32)
    merge_masks = _merge_masks(c)
    neg_a =-jnp.exp(alog_ref[...])
    dtb = dtb_ref[...]
    onorm = onorm_ref[...]
    conv_taps = cw_ref.shape[0]

    def conv_seg(sq, r0, col0):
        win = xc[sq, pl.ds(r0, c + SUBLANES), col0:col0 + HEAD_DIM]
        acc = None
        for w in range(conv_taps):
            shift = conv_taps - 1 - w
            xw = win if shift == 0 else pltpu.roll(win, shift, axis=0)
            term = xw[SUBLANES:, :] * cw_ref[w:w + 1, col0:col0 + HEAD_DIM]
            acc = term if acc is None else acc + term
        return _silu(acc)

    def l2n(v):
        return v * lax.rsqrt(jnp.sum(v * v, axis=-1, keepdims=True) + EPS)

    def chunk_body(seqs, r0):
        gates = {}
        for sq in seqs:
            gate = proj_ref[pl.ds(sq * rows + r0, c), 4 * qk_dim:4 * qk_dim + HEAD_DIM]
            beta_all = jax.nn.sigmoid(gate)
            gpre = gate + dtb
            g_all = neg_a * (jnp.maximum(gpre, 0.0) + _softplus_neg_abs(gpre))
            if valid_rows < c:
                live = lax.broadcasted_iota(jnp.int32, (c, HEAD_DIM), 0) < valid_rows
                beta_all = jnp.where(live, beta_all, 0.0)
                g_all = jnp.where(live, g_all, 0.0)
            gc_all = _dot_exact(cum_mat, g_all)
            gates[sq] = (beta_all, gc_all, gc_all.T)
        units = [(sq, h) for sq in seqs for h in range(n_heads)]
        q = [l2n(conv_seg(sq, r0, h * HEAD_DIM)) * (HEAD_DIM ** -0.5) for sq, h in units]
        k = [l2n(conv_seg(sq, r0, qk_dim + h * HEAD_DIM)) for sq, h in units]
        v = [conv_seg(sq, r0, 2 * qk_dim + h * HEAD_DIM) for sq, h in units]
        beta = [gates[sq][0][:, h:h + 1] for sq, h in units]
        gcol = [gates[sq][1][:, n_heads + h:n_heads + h + 1] for sq, h in units]
        grow = [gates[sq][2][n_heads + h:n_heads + h + 1, :] for sq, h in units]
        n = range(len(units))
        decay = [jnp.exp(jnp.where(lower_incl, gcol[i] - grow[i], -jnp.inf)) for i in n]
        kq = [jnp.concatenate([k[i], q[i]], axis=0).astype(BF16) for i in n]
        kk_qk = [_dot_nt(kq[i], kq[i][0:c, :]) for i in n]
        m = [jnp.where(lower_strict, beta[i] * kk_qk[i][0:c, :] * decay[i], 0.0) for i in n]
        t_inv = _unit_lower_inverses(m, c, merge_masks)
        gam = [jnp.exp(gcol[i]) for i in n]
        rhs = [jnp.concatenate([(beta[i] * gam[i]) * k[i], beta[i] * v[i]], axis=1).astype(BF16) for i in n]
        wu = [_dot(t_inv[i].astype(BF16), rhs[i]) for i in n]
        g_last = [gcol[i][c - 1:c, :] for i in n]
        lhs2 = [jnp.concatenate([kk_qk[i][c:2 * c, :] * decay[i], (k[i] * jnp.exp(g_last[i] - gcol[i])).T],
                                axis=0).astype(BF16) for i in n]
        lhs1 = [jnp.concatenate([wu[i][:, 0:HEAD_DIM], q[i] * gam[i]], axis=0).astype(BF16) for i in n]
        s = [state[sq, h] for sq, h in units]
        ws = [_dot(lhs1[i], s[i].astype(BF16)) for i in n]
        ub = [(wu[i][:, HEAD_DIM:] - ws[i][0:c, :]).astype(BF16) for i in n]
        upd = [_dot(lhs2[i], ub[i]) for i in n]
        for i, (sq, h) in enumerate(units):
            state[sq, h] = jnp.exp(g_last[i]) * s[i] + upd[i][c:, :]
            o = ws[i][c:2 * c, :] + upd[i][0:c, :]
            base = sq * rows + r0
            z = proj_ref[pl.ds(base, c), 3 * qk_dim + h * HEAD_DIM:3 * qk_dim + (h + 1) * HEAD_DIM]
            og[pl.ds(base, c), h * HEAD_DIM:(h + 1) * HEAD_DIM] = _rms(o, onorm) * _silu(z)

    n_chunks = rows // c
    if n_chunks == 1:
        chunk_body(list(range(n_seqs)), 0)
    else:
        def loop_body(ch, carry):
            chunk_body(list(range(n_seqs)), pl.multiple_of(ch * c, c))
            return carry
        lax.fori_loop(0, n_chunks, loop_body, 0)
    xo_ref[...] = x_ref[...] + _dot(og[...].astype(BF16), wout_ref[...])

    @pl.when(blk == n_blocks - 1)
    def _():
        sf_ref[...] = state[...]


def _gdn_layer(x, proj, tail, s0, conv_w, a_log, dt_bias, o_norm, w_out, *, n_seq, seq_rows, chunk, valid_rows):
    t, d = x.shape
    n_heads = d // HEAD_DIM
    np_cols = proj.shape[1]
    rows = min(seq_rows, TOKEN_TILE)
    n_blocks = seq_rows // rows
    n_seqs = 1
    if n_blocks == 1 and n_seq % SAMPLE_SEQS_PER_STEP == 0:
        n_seqs = SAMPLE_SEQS_PER_STEP
    lane_pad = HEAD_DIM - 2 * n_heads
    alog_l = jnp.pad(a_log, (n_heads, lane_pad)).reshape(1, HEAD_DIM)
    dtb_l = jnp.pad(dt_bias, (n_heads, lane_pad)).reshape(1, HEAD_DIM)
    const2 = lambda b, r: (0, 0)
    row_map = lambda b, r: (b * n_blocks + r, 0)
    seq_map4 = lambda b, r: (b, 0, 0, 0)
    return pl.pallas_call(
        functools.partial(_gdn_kernel, n_heads=n_heads, chunk=chunk, rows=rows, valid_rows=valid_rows,
                          n_blocks=n_blocks, n_seqs=n_seqs),
        grid=(n_seq // n_seqs, n_blocks),
        in_specs=[
            pl.BlockSpec((n_seqs * rows, np_cols), row_map),
            pl.BlockSpec((n_seqs, SUBLANES, 3 * d), lambda b, r: (b, 0, 0)),
            pl.BlockSpec((n_seqs, n_heads, HEAD_DIM, HEAD_DIM), seq_map4),
            pl.BlockSpec((n_seqs * rows, d), row_map),
            pl.BlockSpec(conv_w.shape, const2),
            pl.BlockSpec((1, HEAD_DIM), const2),
            pl.BlockSpec((1, HEAD_DIM), const2),
            pl.BlockSpec((1, HEAD_DIM), const2),
            pl.BlockSpec((d, d), const2, pipeline_mode=pl.Buffered(1)),
        ],
        out_specs=[
            pl.BlockSpec((n_seqs * rows, d), row_map),
            pl.BlockSpec((n_seqs, n_heads, HEAD_DIM, HEAD_DIM), seq_map4),
        ],
        out_shape=[jax.ShapeDtypeStruct((t, d), F32),
                   jax.ShapeDtypeStruct((n_seq, n_heads, HEAD_DIM, HEAD_DIM), F32)],
        scratch_shapes=[
            pltpu.VMEM((n_seqs, n_heads, HEAD_DIM, HEAD_DIM), F32),
            pltpu.VMEM((n_seqs, rows + SUBLANES, 3 * d), F32),
            pltpu.VMEM((n_seqs, SUBLANES, 3 * d), F32),
            pltpu.VMEM((n_seqs * rows, d), F32),
        ],
        compiler_params=_params("arbitrary", "arbitrary"),
        name="gdn_mixer",
    )(proj, tail, s0, x, conv_w, alog_l, dtb_l, o_norm.reshape(1, HEAD_DIM), w_out)


def _sb_terms2(z):
    sp = jnp.log2(1.0 + jnp.exp2(-jnp.abs(z)))
    ls_pos = jnp.minimum(z, 0.0) - sp
    return ls_pos, ls_pos - z


def _split_bf16(x):
    hi = x.astype(BF16)
    lo = (x - hi.astype(F32)).astype(BF16)
    return hi, lo


def _sb_prompt_kernel(bias_ref, q_ref, k_ref, v_ref, suf_ref, o_ref, *, tile):
    head = pl.program_id(1)
    step = pl.program_id(2)
    bias = bias_ref[head]
    suf = suf_ref[...]
    earlier = (lax.broadcasted_iota(jnp.int32, (tile, tile), 1) < lax.broadcasted_iota(jnp.int32, (tile, tile), 0))

    def score_tile(q, j, carry, diagonal):
        run, acc = carry
        k0 = pl.multiple_of(j * tile, tile)
        z = _dot_nt(q, k_ref[pl.ds(k0, tile), :]) + bias
        ls_pos, ls_neg = _sb_terms2(z)
        if diagonal:
            ls_neg = jnp.where(earlier, ls_neg, 0.0)
        within = _dot(ls_neg.astype(BF16), suf)
        logw = ls_pos + (run + within)
        if diagonal:
            logw = jnp.where(earlier, logw, -jnp.inf)
        acc = acc + _dot(jnp.exp2(logw).astype(BF16), v_ref[pl.ds(k0, tile), :])
        run = run + jnp.sum(ls_neg, axis=-1, keepdims=True)
        return run, acc

    qa = q_ref[0:tile, :]
    qb = q_ref[tile:2 * tile, :]
    zero = (jnp.zeros((tile, 1), F32), jnp.zeros((tile, HEAD_DIM), F32))
    cb = score_tile(qb, 2 * step + 1, zero, True)
    cb = score_tile(qb, 2 * step, cb, False)
    ca = score_tile(qa, 2 * step, zero, True)

    def body(i, carry):
        j = 2 * (step - i) - 1
        ca, cb = carry
        ca = score_tile(qa, j, ca, False)
        cb = score_tile(qb, j, cb, False)
        ca = score_tile(qa, j - 1, ca, False)
        cb = score_tile(qb, j - 1, cb, False)
        return ca, cb

    ca, cb = lax.fori_loop(0, step, body, (ca, cb))
    o_ref[0:tile, :] = ca[1].astype(o_ref.dtype)
    o_ref[tile:2 * tile, :] = cb[1].astype(o_ref.dtype)


def _sb_prompt_attention(q, k, v, logit_bias, *, n_seq, seq_len):
    t, d = q.shape
    n_heads = d // HEAD_DIM
    tile = ATT_TILE
    assert seq_len % (2 * tile) == 0
    n_qb = seq_len // (2 * tile)
    suf = (jnp.arange(tile)[:, None] > jnp.arange(tile)[None, :]).astype(BF16)
    q_spec = pl.BlockSpec((2 * tile, HEAD_DIM), lambda b, h, i: (b * n_qb + i, h))
    kv_spec = pl.BlockSpec((seq_len, HEAD_DIM), lambda b, h, i: (b, h))
    return pl.pallas_call(
        functools.partial(_sb_prompt_kernel, tile=tile),
        grid=(n_seq, n_heads, n_qb),
        in_specs=[
            pl.BlockSpec(memory_space=pltpu.SMEM),
            q_spec, kv_spec, kv_spec,
            pl.BlockSpec((tile, tile), lambda b, h, i: (0, 0)),
        ],
        out_specs=q_spec,
        out_shape=jax.ShapeDtypeStruct((t, d), BF16),
        compiler_params=_params("parallel", "parallel", "arbitrary"),
        name="sb_prompt_attention",
    )(logit_bias.astype(F32) * LOG2E, q, k, v, suf)


def _sb_sample_kernel(pt_ref, q_ref, kn_ref, vn_ref, *refs, n_heads, n_q, n_new, page, n_steps, per_step):
    del pt_ref
    kc_refs, vc_refs = refs[:per_step], refs[per_step:2 * per_step]
    bias_ref, scan_ref, o_ref, run, acc = refs[2 * per_step:]
    step = pl.program_id(1)
    q = q_ref[0]
    sub = lax.broadcasted_iota(jnp.int32, (n_heads, 1), 0)
    scan = scan_ref[...]

    def log_weights(kflat, n_keys, new_keys):
        lanes = n_keys * n_heads
        n_vreg = lanes // HEAD_DIM
        lane = lax.broadcasted_iota(jnp.int32, (1, lanes), 1)
        head_sel = (lane & (n_heads - 1)) == sub
        z_all = _dot_nt(q, kflat)
        rows = []
        for i in range(n_q):
            zi = jnp.where(head_sel, z_all[i * n_heads:(i + 1) * n_heads, :], 0.0)
            rows.append(jnp.sum(zi, axis=0, keepdims=True))
        z = jnp.concatenate(rows + [jnp.zeros((SUBLANES - n_q, lanes), F32)], axis=0)
        z = z + jnp.concatenate([bias_ref[...]] * n_vreg, axis=1)
        ls_pos, ls_neg = _sb_terms2(z)
        qidx = lax.broadcasted_iota(jnp.int32, (SUBLANES, lanes), 0)
        live = qidx < n_q
        if new_keys:
            key = lax.broadcasted_iota(jnp.int32, (SUBLANES, lanes), 1) // n_heads
            live = live & (key < qidx) & (key < n_new)
        ls_neg = jnp.where(live, ls_neg, 0.0)
        vregs = lambda a: [a[:, v * HEAD_DIM:(v + 1) * HEAD_DIM] for v in range(n_vreg)]
        hi, lo = _split_bf16(jnp.concatenate(vregs(ls_neg), axis=0))
        res = _dot(hi, scan) + _dot(lo, scan)
        after = jnp.zeros((SUBLANES, HEAD_DIM), F32)
        rel = [None] * n_vreg
        for v in reversed(range(n_vreg)):
            blk = res[v * SUBLANES:(v + 1) * SUBLANES, :]
            rel[v] = vregs(ls_pos)[v] + (blk[:, 0:HEAD_DIM] + after)
            after = after + blk[:, HEAD_DIM:]
        return jnp.concatenate(rel, axis=1), after, live, head_sel

    def weighted_values(rel, run_val, live, head_sel, vflat):
        n_vreg = rel.shape[1] // HEAD_DIM
        logw = rel + jnp.concatenate([run_val] * n_vreg, axis=1)
        w = jnp.exp2(jnp.where(live, logw, -jnp.inf))
        w_rows = [jnp.where(head_sel, w[i:i + 1, :], 0.0) for i in range(n_q)]
        return _dot(jnp.concatenate(w_rows, axis=0).astype(BF16), vflat)

    @pl.when(step == 0)
    def _():
        pad_keys = HEAD_DIM // n_heads
        zpad = jnp.zeros(((pad_keys - n_new) * n_heads, HEAD_DIM), BF16)
        rel, total, live, head_sel = log_weights(jnp.concatenate([kn_ref[0], zpad], axis=0), pad_keys, True)
        acc[...] = weighted_values(rel, jnp.zeros((SUBLANES, HEAD_DIM), F32), live, head_sel,
                                   jnp.concatenate([vn_ref[0], zpad], axis=0))
        run[...] = total

    run_val = run[...]
    acc_val = acc[...]
    for i in range(per_step):
        rel, total, live, head_sel = log_weights(kc_refs[i][0].astype(BF16), page, False)
        acc_val = acc_val + weighted_values(rel, run_val, live, head_sel, vc_refs[i][0].astype(BF16))
        run_val = run_val + total
    run[...] = run_val
    acc[...] = acc_val

    @pl.when(step == n_steps - 1)
    def _():
        o_ref[0] = acc_val.astype(o_ref.dtype)


def _sb_sample_attention(q, k_new, v_new, cache_k, cache_v, page_table, logit_bias, *, n_seq, n_q):
    t, d = q.shape
    n_heads = d // HEAD_DIM
    n_phys, page = cache_k.shape[0], cache_k.shape[1]
    n_pages = page_table.shape[1]
    qrows = n_q * n_heads
    q3 = q.reshape(n_seq, qrows, HEAD_DIM)
    kn3 = k_new.reshape(n_seq, qrows, HEAD_DIM)
    vn3 = v_new.reshape(n_seq, qrows, HEAD_DIM)
    kc = cache_k.reshape(n_phys, page * n_heads, HEAD_DIM)
    vc = cache_v.reshape(n_phys, page * n_heads, HEAD_DIM)
    bias_l = (jnp.tile(logit_bias.astype(F32), HEAD_DIM // n_heads) * LOG2E).reshape(1, HEAD_DIM)
    lane = jnp.arange(HEAD_DIM)
    same_head = (lane[:, None] % n_heads) == (lane[None, :] % n_heads)
    later_key = (lane[:, None] // n_heads) > (lane[None, :] // n_heads)
    scan = jnp.concatenate([same_head & later_key, same_head], axis=1).astype(BF16)
    per_step = math.gcd(n_pages, SAMPLE_PAGES_PER_STEP)
    n_steps = n_pages // per_step
    seq3 = lambda b, j, pt: (b, 0, 0)
    const2 = lambda b, j, pt: (0, 0)
    page_specs = [pl.BlockSpec((1, page * n_heads, HEAD_DIM),
                               lambda b, j, pt, i=i: (pt[b, n_pages - 1 - (j * per_step + i)], 0, 0))
                  for i in range(per_step)]
    grid_spec = pltpu.PrefetchScalarGridSpec(
        num_scalar_prefetch=1,
        grid=(n_seq, n_steps),
        in_specs=[pl.BlockSpec((1, qrows, HEAD_DIM), seq3)] * 3 + page_specs + page_specs + [
            pl.BlockSpec((1, HEAD_DIM), const2),
            pl.BlockSpec((HEAD_DIM, 2 * HEAD_DIM), const2),
        ],
        out_specs=pl.BlockSpec((1, qrows, HEAD_DIM), seq3),
        scratch_shapes=[pltpu.VMEM((SUBLANES, HEAD_DIM), F32), pltpu.VMEM((qrows, HEAD_DIM), F32)],
    )
    out = pl.pallas_call(
        functools.partial(_sb_sample_kernel, n_heads=n_heads, n_q=n_q, n_new=n_q, page=page, n_steps=n_steps,
                          per_step=per_step),
        grid_spec=grid_spec,
        out_shape=jax.ShapeDtypeStruct((n_seq, qrows, HEAD_DIM), BF16),
        compiler_params=_params("parallel", "arbitrary"),
        name="sb_sample_attention",
    )(page_table, q3, kn3, vn3, *([kc] * per_step), *([vc] * per_step), bias_l, scan)
    return out.reshape(t, d)


def _trunk(x, tail, s0, past, w, *, n_seq, seq_len):
    d = x.shape[1]
    x = _ffn(x, w["ffn1_norm"][0], w["ffn1_w_gu"][0], w["ffn1_w_down"][0])
    proj = _normproj(x, w["mix_norm"][0], w["gdn_w_in"])
    seq_rows = -(-seq_len // SUBLANES) * SUBLANES
    chunk = min(GDN_CHUNK, seq_rows)
    pad = lambda a: jnp.pad(a.reshape(n_seq, seq_len, -1), ((0, 0), (0, seq_rows - seq_len), (0, 0))
                            ).reshape(n_seq * seq_rows, -1)
    xg, pg = (x, proj) if seq_rows == seq_len else (pad(x), pad(proj))
    xg, state = _gdn_layer(xg, pg, tail, s0, w["gdn_conv_w"], w["gdn_a_log"], w["gdn_dt_bias"], w["gdn_o_norm"],
                           w["gdn_w_out"], n_seq=n_seq, seq_rows=seq_rows, chunk=chunk,
                           valid_rows=min(seq_len, chunk))
    x = xg if seq_rows == seq_len else xg.reshape(n_seq, seq_rows, d)[:, :seq_len].reshape(n_seq * seq_len, d)
    x = _ffn(x, w["ffn2_norm"][0], w["ffn2_w_gu"][0], w["ffn2_w_down"][0])
    k, v, kb, vb = _shared_kv(x, w["kv_norm"], w["w_kv"], w["k_norm"])
    x = _ffn(x, w["ffn1_norm"][1], w["ffn1_w_gu"][1], w["ffn1_w_down"][1])
    q = _sb_q(x, w["mix_norm"][1], w["sb_w_q"], w["sb_q_norm"])
    if past is None:
        att = _sb_prompt_attention(q, kb, vb, w["sb_logit_bias"], n_seq=n_seq, seq_len=seq_len)
    else:
        att = _sb_sample_attention(q, kb, vb, past[0], past[1], past[2], w["sb_logit_bias"], n_seq=n_seq, n_q=seq_len)
    x = _outproj(x, att, w["sb_w_out"])
    x = _ffn(x, w["ffn2_norm"][1], w["ffn2_w_gu"][1], w["ffn2_w_down"][1])
    return x, proj.reshape(n_seq, seq_len, -1), state, k, v


def kernel(x_prompt, x_sample, state_gdn, state_conv, cache_k, cache_v, page_table, ffn1_norm, ffn1_w_gu,
           ffn1_w_down, ffn2_norm, ffn2_w_gu, ffn2_w_down, mix_norm, gdn_w_in, gdn_conv_w, gdn_a_log, gdn_dt_bias,
           gdn_o_norm, gdn_w_out, kv_norm, w_kv, k_norm, sb_w_q, sb_q_norm, sb_logit_bias, sb_w_out):
    bp, seq, d = x_prompt.shape
    bs, dec = x_sample.shape[0], x_sample.shape[1]
    n_heads = d // HEAD_DIM
    hist = state_conv.shape[2]
    assert gdn_w_in.shape[0] == 1 and sb_w_q.shape[0] == 1, "one layer of each mixer kind"
    assert dec <= SUBLANES and hist < SUBLANES and dec >= hist

    in_cols = gdn_w_in.shape[2]
    np_cols = -(-in_cols // HEAD_DIM) * HEAD_DIM
    w = {
        "ffn1_norm": ffn1_norm, "ffn2_norm": ffn2_norm, "mix_norm": mix_norm,
        "ffn1_w_gu": ffn1_w_gu.astype(BF16), "ffn1_w_down": ffn1_w_down.astype(BF16),
        "ffn2_w_gu": ffn2_w_gu.astype(BF16), "ffn2_w_down": ffn2_w_down.astype(BF16),
        "gdn_w_in": jnp.pad(gdn_w_in[0], ((0, 0), (0, np_cols - in_cols))).astype(BF16),
        "gdn_conv_w": gdn_conv_w[0], "gdn_a_log": gdn_a_log[0], "gdn_dt_bias": gdn_dt_bias[0],
        "gdn_o_norm": gdn_o_norm[0], "gdn_w_out": gdn_w_out[0].astype(BF16),
        "kv_norm": kv_norm, "w_kv": w_kv.astype(BF16), "k_norm": k_norm,
        "sb_w_q": sb_w_q[0].astype(BF16), "sb_q_norm": sb_q_norm[0], "sb_logit_bias": sb_logit_bias[0],
        "sb_w_out": sb_w_out[0].astype(BF16),
    }

    zero_tail = jnp.zeros((bp, SUBLANES, 3 * d), F32)
    zero_state = jnp.zeros((bp, n_heads, HEAD_DIM, HEAD_DIM), F32)
    y_p, qkv_p, st_p, k_p, v_p = _trunk(x_prompt.reshape(bp * seq, d), zero_tail, zero_state, None, w,
                                        n_seq=bp, seq_len=seq)

    tail_s = jnp.pad(state_conv[0], ((0, 0), (SUBLANES - hist, 0), (0, 0)))
    y_s, qkv_s, st_s, k_s, v_s = _trunk(x_sample.reshape(bs * dec, d), tail_s, state_gdn[0],
                                        (cache_k, cache_v, page_table), w, n_seq=bs, seq_len=dec)

    conv_p = qkv_p[:, seq - hist:seq, :3 * d][None]
    conv_s = qkv_s[:, dec - hist:dec, :3 * d][None]
    return (y_p.reshape(bp, seq, d), y_s.reshape(bs, dec, d), st_p[None].astype(x_prompt.dtype), conv_p,
            k_p.reshape(bp, seq, n_heads, HEAD_DIM), v_p.reshape(bp, seq, n_heads, HEAD_DIM),
            st_s[None].astype(x_sample.dtype), conv_s,
            k_s.reshape(bs, dec, n_heads, HEAD_DIM), v_s.reshape(bs, dec, n_heads, HEAD_DIM))
```

```python
import functools
import math

import jax
import jax.numpy as jnp
from jax import lax
from jax.experimental import pallas as pl
from jax.experimental.pallas import tpu as pltpu

F32 = jnp.float32
BF16 = jnp.bfloat16

EPS = 1e-6
HEAD_DIM = 128
GDN_CHUNK = 64
SUBLANES = 8
TOKEN_TILE = 512
ATT_TILE = 256
SAMPLE_PAGES_PER_STEP = 8
SAMPLE_SEQS_PER_STEP = 4
VMEM_LIMIT = 56 * 1024 * 1024
PREC_EXACT = lax.Precision.HIGHEST
LOG2E = math.log2(math.e)
Q_SCALE = HEAD_DIM ** -0.5 * LOG2E


def _params(*sem):
    return pltpu.CompilerParams(dimension_semantics=sem, vmem_limit_bytes=VMEM_LIMIT)


def _dot(a, b):
    return jnp.dot(a, b, preferred_element_type=F32)


def _dot_nt(a, b):
    return lax.dot_general(a, b, (((1,), (1,)), ((), ())), preferred_element_type=F32)


def _dot_exact(a, b):
    return jnp.dot(a, b, preferred_element_type=F32, precision=PREC_EXACT)


def _rms(x, w):
    return x * lax.rsqrt(jnp.mean(x * x, axis=-1, keepdims=True) + EPS) * w


def _silu(x):
    return x * jax.nn.sigmoid(x)


def _softplus_neg_abs(x):
    return jnp.log1p(jnp.exp(-jnp.abs(x)))


def _row_tile(n_rows):
    return TOKEN_TILE if n_rows % TOKEN_TILE == 0 else n_rows


def _ffn_kernel(x_ref, nw_ref, wgu_ref, wd_ref, o_ref, *, d_ff, n_split):
    x = x_ref[...]
    h = _rms(x, nw_ref[...]).astype(BF16)
    fc = d_ff // n_split
    acc = jnp.zeros_like(x)
    for c in range(n_split):
        g = _dot(h, wgu_ref[:, c * fc:(c + 1) * fc])
        u = _dot(h, wgu_ref[:, d_ff + c * fc:d_ff + (c + 1) * fc])
        a = (_silu(g) * u).astype(BF16)
        acc = acc + _dot(a, wd_ref[c * fc:(c + 1) * fc, :])
    o_ref[...] = x + 0.5 * acc


def _ffn(x, norm_w, w_gu, w_down):
    t, d = x.shape
    d_ff = w_down.shape[0]
    tm = _row_tile(t)
    n_split = 2 if (d_ff // 2) % 128 == 0 else 1
    const = lambda i: (0, 0)
    return pl.pallas_call(
        functools.partial(_ffn_kernel, d_ff=d_ff, n_split=n_split),
        grid=(t // tm,),
        in_specs=[
            pl.BlockSpec((tm, d), lambda i: (i, 0)),
            pl.BlockSpec((1, d), const),
            pl.BlockSpec((d, 2 * d_ff), const, pipeline_mode=pl.Buffered(1)),
            pl.BlockSpec((d_ff, d), const, pipeline_mode=pl.Buffered(1)),
        ],
        out_specs=pl.BlockSpec((tm, d), lambda i: (i, 0)),
        out_shape=jax.ShapeDtypeStruct((t, d), F32),
        compiler_params=_params("parallel"),
        name="ffn",
    )(x, norm_w.reshape(1, d), w_gu, w_down)


def _normproj_kernel(x_ref, nw_ref, w_ref, o_ref):
    h = _rms(x_ref[...], nw_ref[...]).astype(BF16)
    o_ref[...] = _dot(h, w_ref[...])


def _normproj(x, norm_w, w):
    t, d = x.shape
    n = w.shape[1]
    tm = _row_tile(t)
    const = lambda i: (0, 0)
    return pl.pallas_call(
        _normproj_kernel,
        grid=(t // tm,),
        in_specs=[
            pl.BlockSpec((tm, d), lambda i: (i, 0)),
            pl.BlockSpec((1, d), const),
            pl.BlockSpec((d, n), const, pipeline_mode=pl.Buffered(1)),
        ],
        out_specs=pl.BlockSpec((tm, n), lambda i: (i, 0)),
        out_shape=jax.ShapeDtypeStruct((t, n), F32),
        compiler_params=_params("parallel"),
        name="gdn_in_proj",
    )(x, norm_w.reshape(1, d), w)


def _head_rms(y, w, h):
    seg = y[:, h * HEAD_DIM:(h + 1) * HEAD_DIM]
    return seg * lax.rsqrt(jnp.mean(seg * seg, axis=-1, keepdims=True) + EPS) * w


def _kv_kernel(x_ref, nw_ref, w_ref, kn_ref, k_ref, v_ref, kb_ref, vb_ref, *, n_heads):
    hid = _rms(x_ref[...], nw_ref[...]).astype(BF16)
    kv = _dot(hid, w_ref[...])
    d = n_heads * HEAD_DIM
    for h in range(n_heads):
        sl = slice(h * HEAD_DIM, (h + 1) * HEAD_DIM)
        k = _head_rms(kv, kn_ref[...], h)
        k_ref[:, sl] = k
        kb_ref[:, sl] = k.astype(BF16)
    v = kv[:, d:]
    v_ref[...] = v
    vb_ref[...] = v.astype(BF16)


def _shared_kv(x, kv_norm, w_kv, k_norm):
    t, d = x.shape
    n_heads = d // HEAD_DIM
    tm = _row_tile(t)
    const = lambda i: (0, 0)
    row = pl.BlockSpec((tm, d), lambda i: (i, 0))
    return pl.pallas_call(
        functools.partial(_kv_kernel, n_heads=n_heads),
        grid=(t // tm,),
        in_specs=[
            row,
            pl.BlockSpec((1, d), const),
            pl.BlockSpec((d, 2 * d), const, pipeline_mode=pl.Buffered(1)),
            pl.BlockSpec((1, HEAD_DIM), const),
        ],
        out_specs=[row, row, row, row],
        out_shape=[jax.ShapeDtypeStruct((t, d), F32), jax.ShapeDtypeStruct((t, d), F32),
                   jax.ShapeDtypeStruct((t, d), BF16), jax.ShapeDtypeStruct((t, d), BF16)],
        compiler_params=_params("parallel"),
        name="shared_kv",
    )(x, kv_norm.reshape(1, d), w_kv, k_norm.reshape(1, HEAD_DIM))


def _q_kernel(x_ref, nw_ref, w_ref, qn_ref, q_ref, *, n_heads):
    hid = _rms(x_ref[...], nw_ref[...]).astype(BF16)
    q = _dot(hid, w_ref[...])
    for h in range(n_heads):
        q_ref[:, h * HEAD_DIM:(h + 1) * HEAD_DIM] = (_head_rms(q, qn_ref[...], h) * Q_SCALE).astype(BF16)


def _sb_q(x, norm_w, w_q, q_norm):
    t, d = x.shape
    n_heads = d // HEAD_DIM
    tm = _row_tile(t)
    const = lambda i: (0, 0)
    row = pl.BlockSpec((tm, d), lambda i: (i, 0))
    return pl.pallas_call(
        functools.partial(_q_kernel, n_heads=n_heads),
        grid=(t // tm,),
        in_specs=[
            row,
            pl.BlockSpec((1, d), const),
            pl.BlockSpec((d, d), const, pipeline_mode=pl.Buffered(1)),
            pl.BlockSpec((1, HEAD_DIM), const),
        ],
        out_specs=row,
        out_shape=jax.ShapeDtypeStruct((t, d), BF16),
        compiler_params=_params("parallel"),
        name="sb_q_proj",
    )(x, norm_w.reshape(1, d), w_q, q_norm.reshape(1, HEAD_DIM))


def _outproj_kernel(x_ref, a_ref, w_ref, o_ref):
    o_ref[...] = x_ref[...] + _dot(a_ref[...], w_ref[...])


def _outproj(x, a, w):
    t, d = x.shape
    tm = _row_tile(t)
    row = pl.BlockSpec((tm, d), lambda i: (i, 0))
    return pl.pallas_call(
        _outproj_kernel,
        grid=(t // tm,),
        in_specs=[row, row, pl.BlockSpec((d, d), lambda i: (0, 0), pipeline_mode=pl.Buffered(1))],
        out_specs=row,
        out_shape=jax.ShapeDtypeStruct((t, d), F32),
        compiler_params=_params("parallel"),
        name="sb_out_proj",
    )(x, a, w)


def _unit_lower_inverses(ms, c, merge_masks):
    n_blk = c // SUBLANES
    col_id = lax.broadcasted_iota(jnp.int32, (SUBLANES, c), 1)
    row_id = lax.broadcasted_iota(jnp.int32, (SUBLANES, c), 0)
    m_blk = [[m[b * SUBLANES:(b + 1) * SUBLANES, :] for b in range(n_blk)] for m in ms]
    t_blk = [[(col_id == row_id + b * SUBLANES).astype(F32) for b in range(n_blk)] for _ in ms]
    for j in range(SUBLANES - 1):
        for i in range(len(ms)):
            for b in range(n_blk):
                col = m_blk[i][b][:, b * SUBLANES + j:b * SUBLANES + j + 1]
                t_blk[i][b] = t_blk[i][b] - col * t_blk[i][b][j:j + 1, :]
    ts = [jnp.concatenate(t, axis=0) if n_blk > 1 else t[0] for t in t_blk]
    for off_diag in merge_masks:
        cts = [_dot(jnp.where(off_diag, m, 0.0).astype(BF16), t.astype(BF16)) for m, t in zip(ms, ts)]
        ts = [t - _dot(t.astype(BF16), ct.astype(BF16)) for t, ct in zip(ts, cts)]
    return ts


def _merge_masks(c):
    ri = lax.broadcasted_iota(jnp.int32, (c, c), 0)
    ci = lax.broadcasted_iota(jnp.int32, (c, c), 1)
    masks = []
    shift = int(math.log2(SUBLANES))
    while (1 << shift) < c:
        same_pair = lax.shift_right_logical(ri, shift + 1) == lax.shift_right_logical(ci, shift + 1)
        same_blk = lax.shift_right_logical(ri, shift) == lax.shift_right_logical(ci, shift)
        masks.append(same_pair & jnp.logical_not(same_blk))
        shift += 1
    return masks


def _gdn_kernel(proj_ref, tail_ref, s0_ref, x_ref, cw_ref, alog_ref, dtb_ref, onorm_ref, wout_ref,
                xo_ref, sf_ref, state, xc, tail_s, og, *, n_heads, chunk, rows, valid_rows, n_blocks, n_seqs):
    qk_dim = n_heads * HEAD_DIM
    blk = pl.program_id(1)

    @pl.when(blk == 0)
    def _():
        state[...] = s0_ref[...]
        tail_s[...] = tail_ref[...]

    for sq in range(n_seqs):
        xc[sq, 0:SUBLANES, :] = tail_s[sq]
        xc[sq, SUBLANES:SUBLANES + rows, :] = proj_ref[sq * rows:(sq + 1) * rows, 0:3 * qk_dim]
        tail_s[sq] = xc[sq, rows:rows + SUBLANES, :]

    c = chunk
    ri = lax.broadcasted_iota(jnp.int32, (c, c), 0)
    ci = lax.broadcasted_iota(jnp.int32, (c, c), 1)
    lower_incl = (ri >= ci)
    lower_strict = (ri > ci)
    cum_mat = lower_incl.astype(F32)
    merge_masks = _merge_masks(c)
    neg_a =-jnp.exp(alog_ref[...])
    dtb = dtb_ref[...]
    onorm = onorm_ref[...]
    conv_taps = cw_ref.shape[0]

    def conv_seg(sq, r0, col0):
        win = xc[sq, pl.ds(r0, c + SUBLANES), col0:col0 + HEAD_DIM]
        acc = None
        for w in range(conv_taps):
            shift = conv_taps - 1 - w
            xw = win if shift == 0 else pltpu.roll(win, shift, axis=0)
            term = xw[SUBLANES:, :] * cw_ref[w:w + 1, col0:col0 + HEAD_DIM]
            acc = term if acc is None else acc + term
        return _silu(acc)

    def l2n(v):
        return v * lax.rsqrt(jnp.sum(v * v, axis=-1, keepdims=True) + EPS)

    def chunk_body(seqs, r0):
        gates = {}
        for sq in seqs:
            gate = proj_ref[pl.ds(sq * rows + r0, c), 4 * qk_dim:4 * qk_dim + HEAD_DIM]
            beta_all = jax.nn.sigmoid(gate)
            gpre = gate + dtb
            g_all = neg_a * (jnp.maximum(gpre, 0.0) + _softplus_neg_abs(gpre))
            if valid_rows < c:
                live = lax.broadcasted_iota(jnp.int32, (c, HEAD_DIM), 0) < valid_rows
                beta_all = jnp.where(live, beta_all, 0.0)
                g_all = jnp.where(live, g_all, 0.0)
            gc_all = _dot_exact(cum_mat, g_all)
            gates[sq] = (beta_all, gc_all, gc_all.T)
        units = [(sq, h) for sq in seqs for h in range(n_heads)]
        q = [l2n(conv_seg(sq, r0, h * HEAD_DIM)) * (HEAD_DIM ** -0.5) for sq, h in units]
        k = [l2n(conv_seg(sq, r0, qk_dim + h * HEAD_DIM)) for sq, h in units]
        v = [conv_seg(sq, r0, 2 * qk_dim + h * HEAD_DIM) for sq, h in units]
        beta = [gates[sq][0][:, h:h + 1] for sq, h in units]
        gcol = [gates[sq][1][:, n_heads + h:n_heads + h + 1] for sq, h in units]
        grow = [gates[sq][2][n_heads + h:n_heads + h + 1, :] for sq, h in units]
        n = range(len(units))
        decay = [jnp.exp(jnp.where(lower_incl, gcol[i] - grow[i], -jnp.inf)) for i in n]
        kq = [jnp.concatenate([k[i], q[i]], axis=0).astype(BF16) for i in n]
        kk_qk = [_dot_nt(kq[i], kq[i][0:c, :]) for i in n]
        m = [jnp.where(lower_strict, beta[i] * kk_qk[i][0:c, :] * decay[i], 0.0) for i in n]
        t_inv = _unit_lower_inverses(m, c, merge_masks)
        gam = [jnp.exp(gcol[i]) for i in n]
        rhs = [jnp.concatenate([(beta[i] * gam[i]) * k[i], beta[i] * v[i]], axis=1).astype(BF16) for i in n]
        wu = [_dot(t_inv[i].astype(BF16), rhs[i]) for i in n]
        g_last = [gcol[i][c - 1:c, :] for i in n]
        lhs2 = [jnp.concatenate([kk_qk[i][c:2 * c, :] * decay[i], (k[i] * jnp.exp(g_last[i] - gcol[i])).T],
                                axis=0).astype(BF16) for i in n]
        lhs1 = [jnp.concatenate([wu[i][:, 0:HEAD_DIM], q[i] * gam[i]], axis=0).astype(BF16) for i in n]
        s = [state[sq, h] for sq, h in units]
        ws = [_dot(lhs1[i], s[i].astype(BF16)) for i in n]
        ub = [(wu[i][:, HEAD_DIM:] - ws[i][0:c, :]).astype(BF16) for i in n]
        upd = [_dot(lhs2[i], ub[i]) for i in n]
        for i, (sq, h) in enumerate(units):
            state[sq, h] = jnp.exp(g_last[i]) * s[i] + upd[i][c:, :]
            o = ws[i][c:2 * c, :] + upd[i][0:c, :]
            base = sq * rows + r0
            z = proj_ref[pl.ds(base, c), 3 * qk_dim + h * HEAD_DIM:3 * qk_dim + (h + 1) * HEAD_DIM]
            og[pl.ds(base, c), h * HEAD_DIM:(h + 1) * HEAD_DIM] = _rms(o, onorm) * _silu(z)

    n_chunks = rows // c
    if n_chunks == 1:
        chunk_body(list(range(n_seqs)), 0)
    else:
        def loop_body(ch, carry):
            chunk_body(list(range(n_seqs)), pl.multiple_of(ch * c, c))
            return carry
        lax.fori_loop(0, n_chunks, loop_body, 0)
    xo_ref[...] = x_ref[...] + _dot(og[...].astype(BF16), wout_ref[...])

    @pl.when(blk == n_blocks - 1)
    def _():
        sf_ref[...] = state[...]


def _gdn_layer(x, proj, tail, s0, conv_w, a_log, dt_bias, o_norm, w_out, *, n_seq, seq_rows, chunk, valid_rows):
    t, d = x.shape
    n_heads = d // HEAD_DIM
    np_cols = proj.shape[1]
    rows = min(seq_rows, TOKEN_TILE)
    n_blocks = seq_rows // rows
    n_seqs = 1
    if n_blocks == 1 and n_seq % SAMPLE_SEQS_PER_STEP == 0:
        n_seqs = SAMPLE_SEQS_PER_STEP
    lane_pad = HEAD_DIM - 2 * n_heads
    alog_l = jnp.pad(a_log, (n_heads, lane_pad)).reshape(1, HEAD_DIM)
    dtb_l = jnp.pad(dt_bias, (n_heads, lane_pad)).reshape(1, HEAD_DIM)
    const2 = lambda b, r: (0, 0)
    row_map = lambda b, r: (b * n_blocks + r, 0)
    seq_map4 = lambda b, r: (b, 0, 0, 0)
    return pl.pallas_call(
        functools.partial(_gdn_kernel, n_heads=n_heads, chunk=chunk, rows=rows, valid_rows=valid_rows,
                          n_blocks=n_blocks, n_seqs=n_seqs),
        grid=(n_seq // n_seqs, n_blocks),
        in_specs=[
            pl.BlockSpec((n_seqs * rows, np_cols), row_map),
            pl.BlockSpec((n_seqs, SUBLANES, 3 * d), lambda b, r: (b, 0, 0)),
            pl.BlockSpec((n_seqs, n_heads, HEAD_DIM, HEAD_DIM), seq_map4),
            pl.BlockSpec((n_seqs * rows, d), row_map),
            pl.BlockSpec(conv_w.shape, const2),
            pl.BlockSpec((1, HEAD_DIM), const2),
            pl.BlockSpec((1, HEAD_DIM), const2),
            pl.BlockSpec((1, HEAD_DIM), const2),
            pl.BlockSpec((d, d), const2, pipeline_mode=pl.Buffered(1)),
        ],
        out_specs=[
            pl.BlockSpec((n_seqs * rows, d), row_map),
            pl.BlockSpec((n_seqs, n_heads, HEAD_DIM, HEAD_DIM), seq_map4),
        ],
        out_shape=[jax.ShapeDtypeStruct((t, d), F32),
                   jax.ShapeDtypeStruct((n_seq, n_heads, HEAD_DIM, HEAD_DIM), F32)],
        scratch_shapes=[
            pltpu.VMEM((n_seqs, n_heads, HEAD_DIM, HEAD_DIM), F32),
            pltpu.VMEM((n_seqs, rows + SUBLANES, 3 * d), F32),
            pltpu.VMEM((n_seqs, SUBLANES, 3 * d), F32),
            pltpu.VMEM((n_seqs * rows, d), F32),
        ],
        compiler_params=_params("arbitrary", "arbitrary"),
        name="gdn_mixer",
    )(proj, tail, s0, x, conv_w, alog_l, dtb_l, o_norm.reshape(1, HEAD_DIM), w_out)


def _sb_terms2(z):
    sp = jnp.log2(1.0 + jnp.exp2(-jnp.abs(z)))
    ls_pos = jnp.minimum(z, 0.0) - sp
    return ls_pos, ls_pos - z


def _split_bf16(x):
    hi = x.astype(BF16)
    lo = (x - hi.astype(F32)).astype(BF16)
    return hi, lo


def _sb_prompt_kernel(bias_ref, q_ref, k_ref, v_ref, suf_ref, o_ref, lsp, lsn, rowsum, run, acc, *, tile):
    head = pl.program_id(1)
    step = pl.program_id(2)
    bias = bias_ref[head]
    suf = suf_ref[...]
    earlier = (lax.broadcasted_iota(jnp.int32, (tile, tile), 1) < lax.broadcasted_iota(jnp.int32, (tile, tile), 0))

    def terms(sub, j, buf, slot, diagonal):
        k0 = pl.multiple_of(j * tile, tile)
        z = _dot_nt(q_ref[sub * tile:(sub + 1) * tile, :], k_ref[pl.ds(k0, tile), :]) + bias
        ls_pos, ls_neg = _sb_terms2(z)
        if diagonal:
            ls_neg = jnp.where(earlier, ls_neg, 0.0)
            ls_pos = jnp.where(earlier, ls_pos, -jnp.inf)
        lsp[buf, slot] = ls_pos
        lsn[buf, slot] = ls_neg.astype(BF16)
        rowsum[buf, slot] = jnp.sum(ls_neg, axis=-1, keepdims=True)

    def apply(sub, j, buf, slot):
        k0 = pl.multiple_of(j * tile, tile)
        within = _dot(lsn[buf, slot], suf)
        logw = lsp[buf, slot] + (run[sub] + within)
        acc[sub] += _dot(jnp.exp2(logw).astype(BF16), v_ref[pl.ds(k0, tile), :])
        run[sub] += rowsum[buf, slot]

    run[...] = jnp.zeros_like(run)
    acc[...] = jnp.zeros_like(acc)
    terms(1, 2 * step + 1, 1, 0, True)
    terms(1, 2 * step, 1, 1, False)
    terms(0, 2 * step, 1, 2, True)
    apply(1, 2 * step + 1, 1, 0)
    apply(1, 2 * step, 1, 1)
    apply(0, 2 * step, 1, 2)

    def trip_terms(t, buf):
        j = 2 * (step - t) - 1
        terms(0, j, buf, 0, False)
        terms(1, j, buf, 1, False)
        terms(0, j - 1, buf, 2, False)
        terms(1, j - 1, buf, 3, False)

    def trip_apply(t, buf):
        j = 2 * (step - t) - 1
        apply(0, j, buf, 0)
        apply(1, j, buf, 1)
        apply(0, j - 1, buf, 2)
        apply(1, j - 1, buf, 3)

    @pl.when(step > 0)
    def _():
        trip_terms(0, 0)

    n_pairs = lax.shift_right_logical(jnp.maximum(step - 1, 0), 1)

    def body(i, carry):
        t = 2 * i
        trip_terms(t + 1, 1)
        trip_apply(t, 0)
        trip_terms(t + 2, 0)
        trip_apply(t + 1, 1)
        return carry

    lax.fori_loop(0, n_pairs, body, 0)

    @pl.when((step > 0) & ((step & 1) == 1))
    def _():
        trip_apply(step - 1, 0)

    @pl.when((step > 0) & ((step & 1) == 0))
    def _():
        trip_terms(step - 1, 1)
        trip_apply(step - 2, 0)
        trip_apply(step - 1, 1)

    o_ref[...] = acc[...].reshape(2 * tile, HEAD_DIM).astype(o_ref.dtype)


def _sb_prompt_attention(q, k, v, logit_bias, *, n_seq, seq_len):
    t, d = q.shape
    n_heads = d // HEAD_DIM
    tile = ATT_TILE
    assert seq_len % (2 * tile) == 0
    n_qb = seq_len // (2 * tile)
    suf = (jnp.arange(tile)[:, None] > jnp.arange(tile)[None, :]).astype(BF16)
    q_spec = pl.BlockSpec((2 * tile, HEAD_DIM), lambda b, h, i: (b * n_qb + i, h))
    kv_spec = pl.BlockSpec((seq_len, HEAD_DIM), lambda b, h, i: (b, h))
    return pl.pallas_call(
        functools.partial(_sb_prompt_kernel, tile=tile),
        grid=(n_seq, n_heads, n_qb),
        in_specs=[
            pl.BlockSpec(memory_space=pltpu.SMEM),
            q_spec, kv_spec, kv_spec,
            pl.BlockSpec((tile, tile), lambda b, h, i: (0, 0)),
        ],
        out_specs=q_spec,
        out_shape=jax.ShapeDtypeStruct((t, d), BF16),
        scratch_shapes=[
            pltpu.VMEM((2, 4, tile, tile), F32),
            pltpu.VMEM((2, 4, tile, tile), BF16),
            pltpu.VMEM((2, 4, tile, 1), F32),
            pltpu.VMEM((2, tile, 1), F32),
            pltpu.VMEM((2, tile, HEAD_DIM), F32),
        ],
        compiler_params=_params("parallel", "parallel", "arbitrary"),
        name="sb_prompt_attention",
    )(logit_bias.astype(F32) * LOG2E, q, k, v, suf)


def _sb_sample_kernel(pt_ref, q_ref, kn_ref, vn_ref, *refs, n_heads, n_q, n_new, page, n_steps, per_step):
    del pt_ref
    kc_refs, vc_refs = refs[:per_step], refs[per_step:2 * per_step]
    bias_ref, scan_ref, o_ref, run, acc = refs[2 * per_step:]
    step = pl.program_id(1)
    heads = range(n_heads)
    scan = scan_ref[...]
    qrow = lax.broadcasted_iota(jnp.int32, (SUBLANES, HEAD_DIM), 0)
    kcol = lax.broadcasted_iota(jnp.int32, (SUBLANES, HEAD_DIM), 1)
    live_q = qrow < n_q
    q = [q_ref[0, h].astype(BF16) for h in heads]

    def log_weights(keys, live):
        units = [(g, h) for g in range(len(keys)) for h in heads]
        z = [_dot_nt(q[h], keys[g][h]) + bias_ref[h] for g, h in units]
        lt = [_sb_terms2(zz) for zz in z]
        ls_neg = [jnp.where(live, t[1], 0.0) for t in lt]
        rel, tot = {}, {}
        for g in range(len(keys)):
            hi, lo = _split_bf16(jnp.concatenate(ls_neg[g * n_heads:(g + 1) * n_heads], axis=0))
            res = _dot(hi, scan) + _dot(lo, scan)
            for h in heads:
                blk = res[h * SUBLANES:(h + 1) * SUBLANES, :]
                rel[g, h] = lt[g * n_heads + h][0] + blk[:, 0:HEAD_DIM]
                tot[g, h] = blk[:, HEAD_DIM:]
        return rel, tot

    def weighted_values(rel, run_val, live, values):
        w = jnp.exp2(jnp.where(live, rel + run_val, -jnp.inf)).astype(BF16)
        return _dot(w, values)

    def strided_head(ref, h):
        return ref[0, pl.ds(h, page, stride=n_heads), :].astype(BF16)

    @pl.when(step == 0)
    def _():
        live = live_q & (kcol < qrow) & (kcol < n_new)
        zpad = jnp.zeros((HEAD_DIM - SUBLANES, HEAD_DIM), F32)
        pad = lambda ref, h: jnp.concatenate([ref[0, h], zpad], axis=0).astype(BF16)
        rel, tot = log_weights([[pad(kn_ref, h) for h in heads]], live)
        for h in heads:
            acc[h] = weighted_values(rel[0, h], jnp.zeros((SUBLANES, HEAD_DIM), F32), live, pad(vn_ref, h))
            run[h] = tot[0, h]

    rel, tot = log_weights([[strided_head(kc_refs[g], h) for h in heads] for g in range(per_step)], live_q)
    for h in heads:
        run_val = run[h]
        acc_val = acc[h]
        for g in range(per_step):
            acc_val = acc_val + weighted_values(rel[g, h], run_val, live_q, strided_head(vc_refs[g], h))
            run_val = run_val + tot[g, h]
        run[h] = run_val
        acc[h] = acc_val

    @pl.when(step == n_steps - 1)
    def _():
        o_ref[0] = acc[...]


def _sb_sample_attention(q, k_new, v_new, cache_k, cache_v, page_table, logit_bias, *, n_seq, n_q):
    t, d = q.shape
    n_heads = d // HEAD_DIM
    n_phys, page = cache_k.shape[0], cache_k.shape[1]
    n_pages = page_table.shape[1]
    assert page == HEAD_DIM and n_q <= SUBLANES
    by_head = lambda a: jnp.pad(a.astype(F32).reshape(n_seq, n_q, n_heads, HEAD_DIM).transpose(0, 2, 1, 3),
                                ((0, 0), (0, 0), (0, SUBLANES - n_q), (0, 0)))
    kc = cache_k.reshape(n_phys, page * n_heads, HEAD_DIM)
    vc = cache_v.reshape(n_phys, page * n_heads, HEAD_DIM)
    key = jnp.arange(HEAD_DIM)
    scan = jnp.concatenate([key[:, None] > key[None, :], jnp.ones((HEAD_DIM, HEAD_DIM), bool)], axis=1).astype(BF16)
    per_step = math.gcd(n_pages, SAMPLE_PAGES_PER_STEP)
    n_steps = n_pages // per_step
    seq4 = lambda b, j, pt: (b, 0, 0, 0)
    head_spec = pl.BlockSpec((1, n_heads, SUBLANES, HEAD_DIM), seq4)
    page_specs = [pl.BlockSpec((1, page * n_heads, HEAD_DIM),
                               lambda b, j, pt, i=i: (pt[b, n_pages - 1 - (j * per_step + i)], 0, 0))
                  for i in range(per_step)]
    grid_spec = pltpu.PrefetchScalarGridSpec(
        num_scalar_prefetch=1,
        grid=(n_seq, n_steps),
        in_specs=[head_spec] * 3 + page_specs + page_specs + [
            pl.BlockSpec(memory_space=pltpu.SMEM),
            pl.BlockSpec((HEAD_DIM, 2 * HEAD_DIM), lambda b, j, pt: (0, 0)),
        ],
        out_specs=head_spec,
        scratch_shapes=[pltpu.VMEM((n_heads, SUBLANES, HEAD_DIM), F32)] * 2,
    )
    out = pl.pallas_call(
        functools.partial(_sb_sample_kernel, n_heads=n_heads, n_q=n_q, n_new=n_q, page=page, n_steps=n_steps,
                          per_step=per_step),
        grid_spec=grid_spec,
        out_shape=jax.ShapeDtypeStruct((n_seq, n_heads, SUBLANES, HEAD_DIM), F32),
        compiler_params=_params("parallel", "arbitrary"),
        name="sb_sample_attention",
    )(page_table, by_head(q), by_head(k_new), by_head(v_new), *([kc] * per_step), *([vc] * per_step),
      logit_bias.astype(F32) * LOG2E, scan)
    return out[:, :, :n_q].transpose(0, 2, 1, 3).reshape(t, d).astype(BF16)


def _trunk(x, tail, s0, past, w, *, n_seq, seq_len):
    d = x.shape[1]
    x = _ffn(x, w["ffn1_norm"][0], w["ffn1_w_gu"][0], w["ffn1_w_down"][0])
    proj = _normproj(x, w["mix_norm"][0], w["gdn_w_in"])
    seq_rows = -(-seq_len // SUBLANES) * SUBLANES
    chunk = min(GDN_CHUNK, seq_rows)
    pad = lambda a: jnp.pad(a.reshape(n_seq, seq_len, -1), ((0, 0), (0, seq_rows - seq_len), (0, 0))
                            ).reshape(n_seq * seq_rows, -1)
    xg, pg = (x, proj) if seq_rows == seq_len else (pad(x), pad(proj))
    xg, state = _gdn_layer(xg, pg, tail, s0, w["gdn_conv_w"], w["gdn_a_log"], w["gdn_dt_bias"], w["gdn_o_norm"],
                           w["gdn_w_out"], n_seq=n_seq, seq_rows=seq_rows, chunk=chunk,
                           valid_rows=min(seq_len, chunk))
    x = xg if seq_rows == seq_len else xg.reshape(n_seq, seq_rows, d)[:, :seq_len].reshape(n_seq * seq_len, d)
    x = _ffn(x, w["ffn2_norm"][0], w["ffn2_w_gu"][0], w["ffn2_w_down"][0])
    k, v, kb, vb = _shared_kv(x, w["kv_norm"], w["w_kv"], w["k_norm"])
    x = _ffn(x, w["ffn1_norm"][1], w["ffn1_w_gu"][1], w["ffn1_w_down"][1])
    q = _sb_q(x, w["mix_norm"][1], w["sb_w_q"], w["sb_q_norm"])
    if past is None:
        att = _sb_prompt_attention(q, kb, vb, w["sb_logit_bias"], n_seq=n_seq, seq_len=seq_len)
    else:
        att = _sb_sample_attention(q, kb, vb, past[0], past[1], past[2], w["sb_logit_bias"], n_seq=n_seq, n_q=seq_len)
    x = _outproj(x, att, w["sb_w_out"])
    x = _ffn(x, w["ffn2_norm"][1], w["ffn2_w_gu"][1], w["ffn2_w_down"][1])
    return x, proj.reshape(n_seq, seq_len, -1), state, k, v


def kernel(x_prompt, x_sample, state_gdn, state_conv, cache_k, cache_v, page_table, ffn1_norm, ffn1_w_gu,
           ffn1_w_down, ffn2_norm, ffn2_w_gu, ffn2_w_down, mix_norm, gdn_w_in, gdn_conv_w, gdn_a_log, gdn_dt_bias,
           gdn_o_norm, gdn_w_out, kv_norm, w_kv, k_norm, sb_w_q, sb_q_norm, sb_logit_bias, sb_w_out):
    bp, seq, d = x_prompt.shape
    bs, dec = x_sample.shape[0], x_sample.shape[1]
    n_heads = d // HEAD_DIM
    hist = state_conv.shape[2]
    assert gdn_w_in.shape[0] == 1 and sb_w_q.shape[0] == 1, "one layer of each mixer kind"
    assert dec <= SUBLANES and hist < SUBLANES and dec >= hist

    in_cols = gdn_w_in.shape[2]
    np_cols = -(-in_cols // HEAD_DIM) * HEAD_DIM
    w = {
        "ffn1_norm": ffn1_norm, "ffn2_norm": ffn2_norm, "mix_norm": mix_norm,
        "ffn1_w_gu": ffn1_w_gu.astype(BF16), "ffn1_w_down": ffn1_w_down.astype(BF16),
        "ffn2_w_gu": ffn2_w_gu.astype(BF16), "ffn2_w_down": ffn2_w_down.astype(BF16),
        "gdn_w_in": jnp.pad(gdn_w_in[0], ((0, 0), (0, np_cols - in_cols))).astype(BF16),
        "gdn_conv_w": gdn_conv_w[0], "gdn_a_log": gdn_a_log[0], "gdn_dt_bias": gdn_dt_bias[0],
        "gdn_o_norm": gdn_o_norm[0], "gdn_w_out": gdn_w_out[0].astype(BF16),
        "kv_norm": kv_norm, "w_kv": w_kv.astype(BF16), "k_norm": k_norm,
        "sb_w_q": sb_w_q[0].astype(BF16), "sb_q_norm": sb_q_norm[0], "sb_logit_bias": sb_logit_bias[0],
        "sb_w_out": sb_w_out[0].astype(BF16),
    }

    zero_tail = jnp.zeros((bp, SUBLANES, 3 * d), F32)
    zero_state = jnp.zeros((bp, n_heads, HEAD_DIM, HEAD_DIM), F32)
    y_p, qkv_p, st_p, k_p, v_p = _trunk(x_prompt.reshape(bp * seq, d), zero_tail, zero_state, None, w,
                                        n_seq=bp, seq_len=seq)

    tail_s = jnp.pad(state_conv[0], ((0, 0), (SUBLANES - hist, 0), (0, 0)))
    y_s, qkv_s, st_s, k_s, v_s = _trunk(x_sample.reshape(bs * dec, d), tail_s, state_gdn[0],
                                        (cache_k, cache_v, page_table), w, n_seq=bs, seq_len=dec)

    conv_p = qkv_p[:, seq - hist:seq, :3 * d][None]
    conv_s = qkv_s[:, dec - hist:dec, :3 * d][None]
    return (y_p.reshape(bp, seq, d), y_s.reshape(bs, dec, d), st_p[None].astype(x_prompt.dtype), conv_p,
            k_p.reshape(bp, seq, n_heads, HEAD_DIM), v_p.reshape(bp, seq, n_heads, HEAD_DIM),
            st_s[None].astype(x_sample.dtype), conv_s,
            k_s.reshape(bs, dec, n_heads, HEAD_DIM), v_s.reshape(bs, dec, n_heads, HEAD_DIM))
```

```python
import functools
import math

import jax
import jax.numpy as jnp
from jax import lax
from jax.experimental import pallas as pl
from jax.experimental.pallas import tpu as pltpu

F32 = jnp.float32
BF16 = jnp.bfloat16

EPS = 1e-6
HEAD_DIM = 128
GDN_CHUNK = 64
GDN_CHUNKS_PER_TRIP = 4
SUBLANES = 8
TOKEN_TILE = 512
ATT_TILE = 256
SAMPLE_PAGES_PER_STEP = 8
SAMPLE_SEQS_PER_STEP = 4
VMEM_LIMIT = 56 * 1024 * 1024
PREC_EXACT = lax.Precision.HIGHEST
LOG2E = math.log2(math.e)
Q_SCALE = HEAD_DIM ** -0.5 * LOG2E


def _params(*sem):
    return pltpu.CompilerParams(dimension_semantics=sem, vmem_limit_bytes=VMEM_LIMIT)


def _dot(a, b):
    return jnp.dot(a, b, preferred_element_type=F32)


def _dot_nt(a, b):
    return lax.dot_general(a, b, (((1,), (1,)), ((), ())), preferred_element_type=F32)


def _dot_exact(a, b):
    return jnp.dot(a, b, preferred_element_type=F32, precision=PREC_EXACT)


def _rms(x, w):
    return x * lax.rsqrt(jnp.mean(x * x, axis=-1, keepdims=True) + EPS) * w


def _silu(x):
    return x * jax.nn.sigmoid(x)


def _softplus_neg_abs(x):
    return jnp.log1p(jnp.exp(-jnp.abs(x)))


def _row_tile(n_rows):
    return TOKEN_TILE if n_rows % TOKEN_TILE == 0 else n_rows


def _ffn_kernel(x_ref, nw_ref, wgu_ref, wd_ref, o_ref, *, d_ff, n_split):
    x = x_ref[...]
    h = _rms(x, nw_ref[...]).astype(BF16)
    fc = d_ff // n_split
    acc = jnp.zeros_like(x)
    for c in range(n_split):
        g = _dot(h, wgu_ref[:, c * fc:(c + 1) * fc])
        u = _dot(h, wgu_ref[:, d_ff + c * fc:d_ff + (c + 1) * fc])
        a = (_silu(g) * u).astype(BF16)
        acc = acc + _dot(a, wd_ref[c * fc:(c + 1) * fc, :])
    o_ref[...] = x + 0.5 * acc


def _ffn(x, norm_w, w_gu, w_down, layer):
    t, d = x.shape
    d_ff = w_down.shape[1]
    tm = _row_tile(t)
    n_split = 2 if (d_ff // 2) % 128 == 0 else 1
    of_layer = lambda i: (layer, 0, 0)
    return pl.pallas_call(
        functools.partial(_ffn_kernel, d_ff=d_ff, n_split=n_split),
        grid=(t // tm,),
        in_specs=[
            pl.BlockSpec((tm, d), lambda i: (i, 0)),
            pl.BlockSpec((None, 1, d), of_layer),
            pl.BlockSpec((None, d, 2 * d_ff), of_layer, pipeline_mode=pl.Buffered(1)),
            pl.BlockSpec((None, d_ff, d), of_layer, pipeline_mode=pl.Buffered(1)),
        ],
        out_specs=pl.BlockSpec((tm, d), lambda i: (i, 0)),
        out_shape=jax.ShapeDtypeStruct((t, d), F32),
        compiler_params=_params("parallel"),
        name="ffn",
    )(x, norm_w.reshape(-1, 1, d), w_gu, w_down)


def _normproj_kernel(x_ref, nw_ref, w_ref, o_ref):
    h = _rms(x_ref[...], nw_ref[...]).astype(BF16)
    o_ref[...] = _dot(h, w_ref[...])


def _normproj(x, norm_w, w):
    t, d = x.shape
    n = w.shape[1]
    tm = _row_tile(t)
    const = lambda i: (0, 0)
    return pl.pallas_call(
        _normproj_kernel,
        grid=(t // tm,),
        in_specs=[
            pl.BlockSpec((tm, d), lambda i: (i, 0)),
            pl.BlockSpec((1, d), const),
            pl.BlockSpec((d, n), const, pipeline_mode=pl.Buffered(1)),
        ],
        out_specs=pl.BlockSpec((tm, n), lambda i: (i, 0)),
        out_shape=jax.ShapeDtypeStruct((t, n), F32),
        compiler_params=_params("parallel"),
        name="gdn_in_proj",
    )(x, norm_w.reshape(1, d), w)


def _head_rms(y, w, h):
    seg = y[:, h * HEAD_DIM:(h + 1) * HEAD_DIM]
    return seg * lax.rsqrt(jnp.mean(seg * seg, axis=-1, keepdims=True) + EPS) * w


def _kv_kernel(x_ref, nw_ref, w_ref, kn_ref, k_ref, v_ref, kb_ref, vb_ref, *, n_heads):
    hid = _rms(x_ref[...], nw_ref[...]).astype(BF16)
    kv = _dot(hid, w_ref[...])
    d = n_heads * HEAD_DIM
    for h in range(n_heads):
        sl = slice(h * HEAD_DIM, (h + 1) * HEAD_DIM)
        k = _head_rms(kv, kn_ref[...], h)
        k_ref[:, sl] = k
        kb_ref[:, sl] = k.astype(BF16)
    v = kv[:, d:]
    v_ref[...] = v
    vb_ref[...] = v.astype(BF16)


def _shared_kv(x, kv_norm, w_kv, k_norm):
    t, d = x.shape
    n_heads = d // HEAD_DIM
    tm = _row_tile(t)
    const = lambda i: (0, 0)
    row = pl.BlockSpec((tm, d), lambda i: (i, 0))
    return pl.pallas_call(
        functools.partial(_kv_kernel, n_heads=n_heads),
        grid=(t // tm,),
        in_specs=[
            row,
            pl.BlockSpec((1, d), const),
            pl.BlockSpec((d, 2 * d), const, pipeline_mode=pl.Buffered(1)),
            pl.BlockSpec((1, HEAD_DIM), const),
        ],
        out_specs=[row, row, row, row],
        out_shape=[jax.ShapeDtypeStruct((t, d), F32), jax.ShapeDtypeStruct((t, d), F32),
                   jax.ShapeDtypeStruct((t, d), BF16), jax.ShapeDtypeStruct((t, d), BF16)],
        compiler_params=_params("parallel"),
        name="shared_kv",
    )(x, kv_norm.reshape(1, d), w_kv, k_norm.reshape(1, HEAD_DIM))


def _q_kernel(x_ref, nw_ref, w_ref, qn_ref, q_ref, *, n_heads):
    hid = _rms(x_ref[...], nw_ref[...]).astype(BF16)
    q = _dot(hid, w_ref[...])
    for h in range(n_heads):
        q_ref[:, h * HEAD_DIM:(h + 1) * HEAD_DIM] = (_head_rms(q, qn_ref[...], h) * Q_SCALE).astype(BF16)


def _sb_q(x, norm_w, w_q, q_norm):
    t, d = x.shape
    n_heads = d // HEAD_DIM
    tm = _row_tile(t)
    const = lambda i: (0, 0)
    row = pl.BlockSpec((tm, d), lambda i: (i, 0))
    return pl.pallas_call(
        functools.partial(_q_kernel, n_heads=n_heads),
        grid=(t // tm,),
        in_specs=[
            row,
            pl.BlockSpec((1, d), const),
            pl.BlockSpec((d, d), const, pipeline_mode=pl.Buffered(1)),
            pl.BlockSpec((1, HEAD_DIM), const),
        ],
        out_specs=row,
        out_shape=jax.ShapeDtypeStruct((t, d), BF16),
        compiler_params=_params("parallel"),
        name="sb_q_proj",
    )(x, norm_w.reshape(1, d), w_q, q_norm.reshape(1, HEAD_DIM))


def _outproj_kernel(x_ref, a_ref, w_ref, o_ref):
    o_ref[...] = x_ref[...] + _dot(a_ref[...], w_ref[...])


def _outproj(x, a, w):
    t, d = x.shape
    tm = _row_tile(t)
    row = pl.BlockSpec((tm, d), lambda i: (i, 0))
    return pl.pallas_call(
        _outproj_kernel,
        grid=(t // tm,),
        in_specs=[row, row, pl.BlockSpec((d, d), lambda i: (0, 0), pipeline_mode=pl.Buffered(1))],
        out_specs=row,
        out_shape=jax.ShapeDtypeStruct((t, d), F32),
        compiler_params=_params("parallel"),
        name="sb_out_proj",
    )(x, a, w)


def _unit_lower_inverses(ms, c, merge_masks):
    n_blk = c // SUBLANES
    col_id = lax.broadcasted_iota(jnp.int32, (SUBLANES, c), 1)
    row_id = lax.broadcasted_iota(jnp.int32, (SUBLANES, c), 0)
    m_blk = [[m[b * SUBLANES:(b + 1) * SUBLANES, :] for b in range(n_blk)] for m in ms]
    t_blk = [[(col_id == row_id + b * SUBLANES).astype(F32) for b in range(n_blk)] for _ in ms]
    for j in range(SUBLANES - 1):
        for i in range(len(ms)):
            for b in range(n_blk):
                col = m_blk[i][b][:, b * SUBLANES + j:b * SUBLANES + j + 1]
                t_blk[i][b] = t_blk[i][b] - col * t_blk[i][b][j:j + 1, :]
    ts = [jnp.concatenate(t, axis=0) if n_blk > 1 else t[0] for t in t_blk]
    for off_diag in merge_masks:
        cts = [_dot(jnp.where(off_diag, m, 0.0).astype(BF16), t.astype(BF16)) for m, t in zip(ms, ts)]
        ts = [t - _dot(t.astype(BF16), ct.astype(BF16)) for t, ct in zip(ts, cts)]
    return ts


def _merge_masks(c):
    ri = lax.broadcasted_iota(jnp.int32, (c, c), 0)
    ci = lax.broadcasted_iota(jnp.int32, (c, c), 1)
    masks = []
    shift = int(math.log2(SUBLANES))
    while (1 << shift) < c:
        same_pair = lax.shift_right_logical(ri, shift + 1) == lax.shift_right_logical(ci, shift + 1)
        same_blk = lax.shift_right_logical(ri, shift) == lax.shift_right_logical(ci, shift)
        masks.append(same_pair & jnp.logical_not(same_blk))
        shift += 1
    return masks


def _gdn_kernel(proj_ref, tail_ref, s0_ref, x_ref, cw_ref, alog_ref, dtb_ref, onorm_ref, wout_ref,
                xo_ref, sf_ref, state, xc, tail_s, og, *, n_heads, chunk, rows, valid_rows, n_blocks, n_seqs):
    qk_dim = n_heads * HEAD_DIM
    blk = pl.program_id(1)

    @pl.when(blk == 0)
    def _():
        state[...] = s0_ref[...]
        tail_s[...] = tail_ref[...]

    for sq in range(n_seqs):
        xc[sq, 0:SUBLANES, :] = tail_s[sq]
        xc[sq, SUBLANES:SUBLANES + rows, :] = proj_ref[sq * rows:(sq + 1) * rows, 0:3 * qk_dim]
        tail_s[sq] = xc[sq, rows:rows + SUBLANES, :]

    c = chunk
    ri = lax.broadcasted_iota(jnp.int32, (c, c), 0)
    ci = lax.broadcasted_iota(jnp.int32, (c, c), 1)
    lower_incl = (ri >= ci)
    lower_strict = (ri > ci)
    cum_mat = lower_incl.astype(F32)
    merge_masks = _merge_masks(c)
    neg_a =-jnp.exp(alog_ref[...])
    dtb = dtb_ref[...]
    onorm = onorm_ref[...]
    conv_taps = cw_ref.shape[0]

    def conv_seg(sq, r0, col0):
        win = xc[sq, pl.ds(r0, c + SUBLANES), col0:col0 + HEAD_DIM]
        acc = None
        for w in range(conv_taps):
            shift = conv_taps - 1 - w
            xw = win if shift == 0 else pltpu.roll(win, shift, axis=0)
            term = xw[SUBLANES:, :] * cw_ref[w:w + 1, col0:col0 + HEAD_DIM]
            acc = term if acc is None else acc + term
        return _silu(acc)

    def l2n(v):
        return v * lax.rsqrt(jnp.sum(v * v, axis=-1, keepdims=True) + EPS)

    def chunk_body(seqs, r0s):
        gates = {}
        for sq, ci in [(sq, ci) for ci in range(len(r0s)) for sq in seqs]:
            gate = proj_ref[pl.ds(sq * rows + r0s[ci], c), 4 * qk_dim:4 * qk_dim + HEAD_DIM]
            beta_all = jax.nn.sigmoid(gate)
            gpre = gate + dtb
            g_all = neg_a * (jnp.maximum(gpre, 0.0) + _softplus_neg_abs(gpre))
            if valid_rows < c:
                live = lax.broadcasted_iota(jnp.int32, (c, HEAD_DIM), 0) < valid_rows
                beta_all = jnp.where(live, beta_all, 0.0)
                g_all = jnp.where(live, g_all, 0.0)
            gc_all = _dot_exact(cum_mat, g_all)
            gates[sq, ci] = (beta_all, gc_all, gc_all.T)
        units = [(sq, h, ci) for ci in range(len(r0s)) for sq in seqs for h in range(n_heads)]
        q = [l2n(conv_seg(sq, r0s[ci], h * HEAD_DIM)) * (HEAD_DIM ** -0.5) for sq, h, ci in units]
        k = [l2n(conv_seg(sq, r0s[ci], qk_dim + h * HEAD_DIM)) for sq, h, ci in units]
        v = [conv_seg(sq, r0s[ci], 2 * qk_dim + h * HEAD_DIM) for sq, h, ci in units]
        lanes = lambda col: jnp.broadcast_to(col, (c, HEAD_DIM))
        beta = [lanes(gates[sq, ci][0][:, h:h + 1]) for sq, h, ci in units]
        gcol = [lanes(gates[sq, ci][1][:, n_heads + h:n_heads + h + 1]) for sq, h, ci in units]
        grow = [gates[sq, ci][2][n_heads + h:n_heads + h + 1, :] for sq, h, ci in units]
        n = range(len(units))
        decay = [jnp.exp(jnp.where(lower_incl, gcol[i][:, 0:c] - grow[i], -jnp.inf)) for i in n]
        kq = [jnp.concatenate([k[i], q[i]], axis=0).astype(BF16) for i in n]
        kk_qk = [_dot_nt(kq[i], kq[i][0:c, :]) for i in n]
        m = [jnp.where(lower_strict, beta[i][:, 0:c] * kk_qk[i][0:c, :] * decay[i], 0.0) for i in n]
        t_inv = _unit_lower_inverses(m, c, merge_masks)
        gam = [jnp.exp(gcol[i]) for i in n]
        rhs = [jnp.concatenate([(beta[i] * gam[i]) * k[i], beta[i] * v[i]], axis=1).astype(BF16) for i in n]
        wu = [_dot(t_inv[i].astype(BF16), rhs[i]) for i in n]
        g_last = [gcol[i][c - 1:c, :] for i in n]
        lhs2 = [jnp.concatenate([kk_qk[i][c:2 * c, :] * decay[i], (k[i] * jnp.exp(g_last[i] - gcol[i])).T],
                                axis=0).astype(BF16) for i in n]
        lhs1 = [jnp.concatenate([wu[i][:, 0:HEAD_DIM], q[i] * gam[i]], axis=0).astype(BF16) for i in n]
        for cj in range(len(r0s)):
            mine = [i for i in n if units[i][2] == cj]
            s = {i: state[units[i][0], units[i][1]] for i in mine}
            ws = {i: _dot(lhs1[i], s[i].astype(BF16)) for i in mine}
            ub = {i: (wu[i][:, HEAD_DIM:] - ws[i][0:c, :]).astype(BF16) for i in mine}
            upd = {i: _dot(lhs2[i], ub[i]) for i in mine}
            for i in mine:
                sq, h, _ = units[i]
                state[sq, h] = jnp.exp(g_last[i]) * s[i] + upd[i][c:, :]
                o = ws[i][c:2 * c, :] + upd[i][0:c, :]
                base = sq * rows + r0s[cj]
                z = proj_ref[pl.ds(base, c), 3 * qk_dim + h * HEAD_DIM:3 * qk_dim + (h + 1) * HEAD_DIM]
                og[pl.ds(base, c), h * HEAD_DIM:(h + 1) * HEAD_DIM] = _rms(o, onorm) * _silu(z)

    n_chunks = rows // c
    per_trip = GDN_CHUNKS_PER_TRIP if n_chunks % GDN_CHUNKS_PER_TRIP == 0 else 1
    if n_chunks == 1:
        chunk_body(list(range(n_seqs)), [0])
    else:
        def loop_body(trip, carry):
            r0 = pl.multiple_of(trip * (per_trip * c), per_trip * c)
            chunk_body(list(range(n_seqs)), [r0 + j * c for j in range(per_trip)])
            return carry
        lax.fori_loop(0, n_chunks // per_trip, loop_body, 0)
    xo_ref[...] = x_ref[...] + _dot(og[...].astype(BF16), wout_ref[...])

    @pl.when(blk == n_blocks - 1)
    def _():
        sf_ref[...] = state[...]


def _gdn_layer(x, proj, tail, s0, conv_w, a_log, dt_bias, o_norm, w_out, *, n_seq, seq_rows, chunk, valid_rows):
    t, d = x.shape
    n_heads = d // HEAD_DIM
    np_cols = proj.shape[1]
    rows = min(seq_rows, TOKEN_TILE)
    n_blocks = seq_rows // rows
    n_seqs = 1
    if n_blocks == 1 and n_seq % SAMPLE_SEQS_PER_STEP == 0:
        n_seqs = SAMPLE_SEQS_PER_STEP
    lane_pad = HEAD_DIM - 2 * n_heads
    alog_l = jnp.pad(a_log, (n_heads, lane_pad)).reshape(1, HEAD_DIM)
    dtb_l = jnp.pad(dt_bias, (n_heads, lane_pad)).reshape(1, HEAD_DIM)
    const2 = lambda b, r: (0, 0)
    row_map = lambda b, r: (b * n_blocks + r, 0)
    seq_map4 = lambda b, r: (b, 0, 0, 0)
    return pl.pallas_call(
        functools.partial(_gdn_kernel, n_heads=n_heads, chunk=chunk, rows=rows, valid_rows=valid_rows,
                          n_blocks=n_blocks, n_seqs=n_seqs),
        grid=(n_seq // n_seqs, n_blocks),
        in_specs=[
            pl.BlockSpec((n_seqs * rows, np_cols), row_map),
            pl.BlockSpec((n_seqs, SUBLANES, 3 * d), lambda b, r: (b, 0, 0)),
            pl.BlockSpec((n_seqs, n_heads, HEAD_DIM, HEAD_DIM), seq_map4),
            pl.BlockSpec((n_seqs * rows, d), row_map),
            pl.BlockSpec(conv_w.shape, const2),
            pl.BlockSpec((1, HEAD_DIM), const2),
            pl.BlockSpec((1, HEAD_DIM), const2),
            pl.BlockSpec((1, HEAD_DIM), const2),
            pl.BlockSpec((d, d), const2, pipeline_mode=pl.Buffered(1)),
        ],
        out_specs=[
            pl.BlockSpec((n_seqs * rows, d), row_map),
            pl.BlockSpec((n_seqs, n_heads, HEAD_DIM, HEAD_DIM), seq_map4),
        ],
        out_shape=[jax.ShapeDtypeStruct((t, d), F32),
                   jax.ShapeDtypeStruct((n_seq, n_heads, HEAD_DIM, HEAD_DIM), F32)],
        scratch_shapes=[
            pltpu.VMEM((n_seqs, n_heads, HEAD_DIM, HEAD_DIM), F32),
            pltpu.VMEM((n_seqs, rows + SUBLANES, 3 * d), F32),
            pltpu.VMEM((n_seqs, SUBLANES, 3 * d), F32),
            pltpu.VMEM((n_seqs * rows, d), F32),
        ],
        compiler_params=_params("arbitrary", "arbitrary"),
        name="gdn_mixer",
    )(proj, tail, s0, x, conv_w, alog_l, dtb_l, o_norm.reshape(1, HEAD_DIM), w_out)


def _sb_terms2(z):
    sp = jnp.log2(1.0 + jnp.exp2(-jnp.abs(z)))
    ls_pos = jnp.minimum(z, 0.0) - sp
    return ls_pos, ls_pos - z


def _split_bf16(x):
    hi = x.astype(BF16)
    lo = (x - hi.astype(F32)).astype(BF16)
    return hi, lo


def _sb_prompt_kernel(bias_ref, q_ref, k_ref, v_ref, suf_ref, o_ref, lsp, lsn, rowsum, run, acc, *, tile):
    head = pl.program_id(1)
    step = pl.program_id(2)
    bias = bias_ref[head]
    suf = suf_ref[...]
    earlier = (lax.broadcasted_iota(jnp.int32, (tile, tile), 1) < lax.broadcasted_iota(jnp.int32, (tile, tile), 0))

    def terms(sub, j, buf, slot, diagonal):
        k0 = pl.multiple_of(j * tile, tile)
        z = _dot_nt(q_ref[sub * tile:(sub + 1) * tile, :], k_ref[pl.ds(k0, tile), :]) + bias
        ls_pos, ls_neg = _sb_terms2(z)
        if diagonal:
            ls_neg = jnp.where(earlier, ls_neg, 0.0)
            ls_pos = jnp.where(earlier, ls_pos, -jnp.inf)
        lsp[buf, slot] = ls_pos
        lsn[buf, slot] = ls_neg.astype(BF16)
        rowsum[buf, slot] = jnp.sum(ls_neg, axis=-1, keepdims=True)

    def apply(sub, j, buf, slot):
        k0 = pl.multiple_of(j * tile, tile)
        within = _dot(lsn[buf, slot], suf)
        logw = lsp[buf, slot] + (run[sub] + within)
        acc[sub] += _dot(jnp.exp2(logw).astype(BF16), v_ref[pl.ds(k0, tile), :])
        run[sub] += rowsum[buf, slot]

    run[...] = jnp.zeros_like(run)
    acc[...] = jnp.zeros_like(acc)
    terms(1, 2 * step + 1, 1, 0, True)
    terms(1, 2 * step, 1, 1, False)
    terms(0, 2 * step, 1, 2, True)

    def trip_terms(t, buf):
        j = jnp.maximum(2 * (step - t) - 1, 1)
        terms(0, j, buf, 0, False)
        terms(1, j, buf, 1, False)
        terms(0, j - 1, buf, 2, False)
        terms(1, j - 1, buf, 3, False)

    def trip_apply(t, buf):
        j = 2 * (step - t) - 1
        apply(0, j, buf, 0)
        apply(1, j, buf, 1)
        apply(0, j - 1, buf, 2)
        apply(1, j - 1, buf, 3)

    trip_terms(0, 0)
    apply(1, 2 * step + 1, 1, 0)
    apply(1, 2 * step, 1, 1)
    apply(0, 2 * step, 1, 2)

    n_pairs = lax.shift_right_logical(jnp.maximum(step - 1, 0), 1)

    def body(i, carry):
        t = 2 * i
        trip_terms(t + 1, 1)
        trip_apply(t, 0)
        trip_terms(t + 2, 0)
        trip_apply(t + 1, 1)
        return carry

    lax.fori_loop(0, n_pairs, body, 0)

    @pl.when((step > 0) & ((step & 1) == 1))
    def _():
        trip_apply(step - 1, 0)

    @pl.when((step > 0) & ((step & 1) == 0))
    def _():
        trip_terms(step - 1, 1)
        trip_apply(step - 2, 0)
        trip_apply(step - 1, 1)

    o_ref[...] = acc[...].reshape(2 * tile, HEAD_DIM).astype(o_ref.dtype)


def _sb_prompt_attention(q, k, v, logit_bias, *, n_seq, seq_len):
    t, d = q.shape
    n_heads = d // HEAD_DIM
    tile = ATT_TILE
    assert seq_len % (2 * tile) == 0
    n_qb = seq_len // (2 * tile)
    suf = (jnp.arange(tile)[:, None] > jnp.arange(tile)[None, :]).astype(BF16)
    q_spec = pl.BlockSpec((2 * tile, HEAD_DIM), lambda b, h, i: (b * n_qb + i, h))
    kv_spec = pl.BlockSpec((seq_len, HEAD_DIM), lambda b, h, i: (b, h))
    return pl.pallas_call(
        functools.partial(_sb_prompt_kernel, tile=tile),
        grid=(n_seq, n_heads, n_qb),
        in_specs=[
            pl.BlockSpec(memory_space=pltpu.SMEM),
            q_spec, kv_spec, kv_spec,
            pl.BlockSpec((tile, tile), lambda b, h, i: (0, 0)),
        ],
        out_specs=q_spec,
        out_shape=jax.ShapeDtypeStruct((t, d), BF16),
        scratch_shapes=[
            pltpu.VMEM((2, 4, tile, tile), F32),
            pltpu.VMEM((2, 4, tile, tile), BF16),
            pltpu.VMEM((2, 4, tile, 1), F32),
            pltpu.VMEM((2, tile, 1), F32),
            pltpu.VMEM((2, tile, HEAD_DIM), F32),
        ],
        compiler_params=_params("parallel", "parallel", "arbitrary"),
        name="sb_prompt_attention",
    )(logit_bias.astype(F32) * LOG2E, q, k, v, suf)


def _sb_sample_kernel(pt_ref, q_ref, kn_ref, vn_ref, *refs, n_heads, n_q, n_new, page, n_steps, per_step):
    del pt_ref
    kc_refs, vc_refs = refs[:per_step], refs[per_step:2 * per_step]
    bias_ref, scan_ref, o_ref, run, acc = refs[2 * per_step:]
    step = pl.program_id(1)
    heads = range(n_heads)
    scan = scan_ref[...]
    qrow = lax.broadcasted_iota(jnp.int32, (SUBLANES, HEAD_DIM), 0)
    kcol = lax.broadcasted_iota(jnp.int32, (SUBLANES, HEAD_DIM), 1)
    live_q = qrow < n_q
    q = [q_ref[0, h].astype(BF16) for h in heads]

    def log_weights(keys, live):
        units = [(g, h) for g in range(len(keys)) for h in heads]
        z = [_dot_nt(q[h], keys[g][h]) + bias_ref[h] for g, h in units]
        lt = [_sb_terms2(zz) for zz in z]
        ls_neg = [jnp.where(live, t[1], 0.0) for t in lt]
        rel, tot = {}, {}
        for g in range(len(keys)):
            hi, lo = _split_bf16(jnp.concatenate(ls_neg[g * n_heads:(g + 1) * n_heads], axis=0))
            res = _dot(hi, scan) + _dot(lo, scan)
            for h in heads:
                blk = res[h * SUBLANES:(h + 1) * SUBLANES, :]
                rel[g, h] = lt[g * n_heads + h][0] + blk[:, 0:HEAD_DIM]
                tot[g, h] = blk[:, HEAD_DIM:]
        return rel, tot

    def weighted_values(rel, run_val, live, values):
        w = jnp.exp2(jnp.where(live, rel + run_val, -jnp.inf)).astype(BF16)
        return _dot(w, values)

    def strided_head(ref, h):
        return ref[0, pl.ds(h, page, stride=n_heads), :].astype(BF16)

    @pl.when(step == 0)
    def _():
        live = live_q & (kcol < qrow) & (kcol < n_new)
        zpad = jnp.zeros((HEAD_DIM - SUBLANES, HEAD_DIM), F32)
        pad = lambda ref, h: jnp.concatenate([ref[0, h], zpad], axis=0).astype(BF16)
        rel, tot = log_weights([[pad(kn_ref, h) for h in heads]], live)
        for h in heads:
            acc[h] = weighted_values(rel[0, h], jnp.zeros((SUBLANES, HEAD_DIM), F32), live, pad(vn_ref, h))
            run[h] = tot[0, h]

    rel, tot = log_weights([[strided_head(kc_refs[g], h) for h in heads] for g in range(per_step)], live_q)
    for h in heads:
        run_val = run[h]
        acc_val = acc[h]
        for g in range(per_step):
            acc_val = acc_val + weighted_values(rel[g, h], run_val, live_q, strided_head(vc_refs[g], h))
            run_val = run_val + tot[g, h]
        run[h] = run_val
        acc[h] = acc_val

    @pl.when(step == n_steps - 1)
    def _():
        o_ref[0] = acc[...]


def _sb_sample_attention(q, k_new, v_new, cache_k, cache_v, page_table, logit_bias, *, n_seq, n_q):
    t, d = q.shape
    n_heads = d // HEAD_DIM
    n_phys, page = cache_k.shape[0], cache_k.shape[1]
    n_pages = page_table.shape[1]
    assert page == HEAD_DIM and n_q <= SUBLANES
    by_head = lambda a: jnp.pad(a.astype(F32).reshape(n_seq, n_q, n_heads, HEAD_DIM).transpose(0, 2, 1, 3),
                                ((0, 0), (0, 0), (0, SUBLANES - n_q), (0, 0)))
    kc = cache_k.reshape(n_phys, page * n_heads, HEAD_DIM)
    vc = cache_v.reshape(n_phys, page * n_heads, HEAD_DIM)
    key = jnp.arange(HEAD_DIM)
    scan = jnp.concatenate([key[:, None] > key[None, :], jnp.ones((HEAD_DIM, HEAD_DIM), bool)], axis=1).astype(BF16)
    per_step = math.gcd(n_pages, SAMPLE_PAGES_PER_STEP)
    n_steps = n_pages // per_step
    seq4 = lambda b, j, pt: (b, 0, 0, 0)
    head_spec = pl.BlockSpec((1, n_heads, SUBLANES, HEAD_DIM), seq4)
    page_specs = [pl.BlockSpec((1, page * n_heads, HEAD_DIM),
                               lambda b, j, pt, i=i: (pt[b, n_pages - 1 - (j * per_step + i)], 0, 0))
                  for i in range(per_step)]
    grid_spec = pltpu.PrefetchScalarGridSpec(
        num_scalar_prefetch=1,
        grid=(n_seq, n_steps),
        in_specs=[head_spec] * 3 + page_specs + page_specs + [
            pl.BlockSpec(memory_space=pltpu.SMEM),
            pl.BlockSpec((HEAD_DIM, 2 * HEAD_DIM), lambda b, j, pt: (0, 0)),
        ],
        out_specs=head_spec,
        scratch_shapes=[pltpu.VMEM((n_heads, SUBLANES, HEAD_DIM), F32)] * 2,
    )
    out = pl.pallas_call(
        functools.partial(_sb_sample_kernel, n_heads=n_heads, n_q=n_q, n_new=n_q, page=page, n_steps=n_steps,
                          per_step=per_step),
        grid_spec=grid_spec,
        out_shape=jax.ShapeDtypeStruct((n_seq, n_heads, SUBLANES, HEAD_DIM), F32),
        compiler_params=_params("parallel", "arbitrary"),
        name="sb_sample_attention",
    )(page_table, by_head(q), by_head(k_new), by_head(v_new), *([kc] * per_step), *([vc] * per_step),
      logit_bias.astype(F32) * LOG2E, scan)
    return out[:, :, :n_q].transpose(0, 2, 1, 3).reshape(t, d).astype(BF16)


def _trunk(x, tail, s0, past, w, *, n_seq, seq_len):
    d = x.shape[1]
    x = _ffn(x, w["ffn1_norm"], w["ffn1_w_gu"], w["ffn1_w_down"], 0)
    proj = _normproj(x, w["mix_norm"][0], w["gdn_w_in"])
    seq_rows = -(-seq_len // SUBLANES) * SUBLANES
    chunk = min(GDN_CHUNK, seq_rows)
    pad = lambda a: jnp.pad(a.reshape(n_seq, seq_len, -1), ((0, 0), (0, seq_rows - seq_len), (0, 0))
                            ).reshape(n_seq * seq_rows, -1)
    xg, pg = (x, proj) if seq_rows == seq_len else (pad(x), pad(proj))
    xg, state = _gdn_layer(xg, pg, tail, s0, w["gdn_conv_w"], w["gdn_a_log"], w["gdn_dt_bias"], w["gdn_o_norm"],
                           w["gdn_w_out"], n_seq=n_seq, seq_rows=seq_rows, chunk=chunk,
                           valid_rows=min(seq_len, chunk))
    x = xg if seq_rows == seq_len else xg.reshape(n_seq, seq_rows, d)[:, :seq_len].reshape(n_seq * seq_len, d)
    x = _ffn(x, w["ffn2_norm"], w["ffn2_w_gu"], w["ffn2_w_down"], 0)
    k, v, kb, vb = _shared_kv(x, w["kv_norm"], w["w_kv"], w["k_norm"])
    x = _ffn(x, w["ffn1_norm"], w["ffn1_w_gu"], w["ffn1_w_down"], 1)
    q = _sb_q(x, w["mix_norm"][1], w["sb_w_q"], w["sb_q_norm"])
    if past is None:
        att = _sb_prompt_attention(q, kb, vb, w["sb_logit_bias"], n_seq=n_seq, seq_len=seq_len)
    else:
        att = _sb_sample_attention(q, kb, vb, past[0], past[1], past[2], w["sb_logit_bias"], n_seq=n_seq, n_q=seq_len)
    x = _outproj(x, att, w["sb_w_out"])
    x = _ffn(x, w["ffn2_norm"], w["ffn2_w_gu"], w["ffn2_w_down"], 1)
    return x, proj.reshape(n_seq, seq_len, -1), state, k, v


def kernel(x_prompt, x_sample, state_gdn, state_conv, cache_k, cache_v, page_table, ffn1_norm, ffn1_w_gu,
           ffn1_w_down, ffn2_norm, ffn2_w_gu, ffn2_w_down, mix_norm, gdn_w_in, gdn_conv_w, gdn_a_log, gdn_dt_bias,
           gdn_o_norm, gdn_w_out, kv_norm, w_kv, k_norm, sb_w_q, sb_q_norm, sb_logit_bias, sb_w_out):
    bp, seq, d = x_prompt.shape
    bs, dec = x_sample.shape[0], x_sample.shape[1]
    n_heads = d // HEAD_DIM
    hist = state_conv.shape[2]
    assert gdn_w_in.shape[0] == 1 and sb_w_q.shape[0] == 1, "one layer of each mixer kind"
    assert dec <= SUBLANES and hist < SUBLANES and dec >= hist

    in_cols = gdn_w_in.shape[2]
    np_cols = -(-in_cols // HEAD_DIM) * HEAD_DIM
    w = {
        "ffn1_norm": ffn1_norm, "ffn2_norm": ffn2_norm, "mix_norm": mix_norm,
        "ffn1_w_gu": ffn1_w_gu.astype(BF16), "ffn1_w_down": ffn1_w_down.astype(BF16),
        "ffn2_w_gu": ffn2_w_gu.astype(BF16), "ffn2_w_down": ffn2_w_down.astype(BF16),
        "gdn_w_in": jnp.pad(gdn_w_in[0], ((0, 0), (0, np_cols - in_cols))).astype(BF16),
        "gdn_conv_w": gdn_conv_w[0], "gdn_a_log": gdn_a_log[0], "gdn_dt_bias": gdn_dt_bias[0],
        "gdn_o_norm": gdn_o_norm[0], "gdn_w_out": gdn_w_out[0].astype(BF16),
        "kv_norm": kv_norm, "w_kv": w_kv.astype(BF16), "k_norm": k_norm,
        "sb_w_q": sb_w_q[0].astype(BF16), "sb_q_norm": sb_q_norm[0], "sb_logit_bias": sb_logit_bias[0],
        "sb_w_out": sb_w_out[0].astype(BF16),
    }

    zero_tail = jnp.zeros((bp, SUBLANES, 3 * d), F32)
    zero_state = jnp.zeros((bp, n_heads, HEAD_DIM, HEAD_DIM), F32)
    y_p, qkv_p, st_p, k_p, v_p = _trunk(x_prompt.reshape(bp * seq, d), zero_tail, zero_state, None, w,
                                        n_seq=bp, seq_len=seq)

    tail_s = jnp.pad(state_conv[0], ((0, 0), (SUBLANES - hist, 0), (0, 0)))
    y_s, qkv_s, st_s, k_s, v_s = _trunk(x_sample.reshape(bs * dec, d), tail_s, state_gdn[0],
                                        (cache_k, cache_v, page_table), w, n_seq=bs, seq_len=dec)

    conv_p = qkv_p[:, seq - hist:seq, :3 * d][None]
    conv_s = qkv_s[:, dec - hist:dec, :3 * d][None]
    return (y_p.reshape(bp, seq, d), y_s.reshape(bs, dec, d), st_p[None].astype(x_prompt.dtype), conv_p,
            k_p.reshape(bp, seq, n_heads, HEAD_DIM), v_p.reshape(bp, seq, n_heads, HEAD_DIM),
            st_s[None].astype(x_sample.dtype), conv_s,
            k_s.reshape(bs, dec, n_heads, HEAD_DIM), v_s.reshape(bs, dec, n_heads, HEAD_DIM))
```

```python
import functools
import math

import jax
import jax.numpy as jnp
from jax import lax
from jax.experimental import pallas as pl
from jax.experimental.pallas import tpu as pltpu

F32 = jnp.float32
BF16 = jnp.bfloat16

EPS = 1e-6
HEAD_DIM = 128
GDN_CHUNK = 64
GDN_CHUNKS_PER_TRIP = 4
SUBLANES = 8
TOKEN_TILE = 512
ATT_TILE = 256
ATT_TILES_PER_TRIP = 4
SAMPLE_PAGES_PER_STEP = 8
SAMPLE_SEQS_PER_STEP = 4
VMEM_LIMIT = 56 * 1024 * 1024
PREC_EXACT = lax.Precision.HIGHEST
LOG2E = math.log2(math.e)
Q_SCALE = HEAD_DIM ** -0.5 * LOG2E


def _params(*sem):
    return pltpu.CompilerParams(dimension_semantics=sem, vmem_limit_bytes=VMEM_LIMIT)


def _dot(a, b):
    return jnp.dot(a, b, preferred_element_type=F32)


def _dot_nt(a, b):
    return lax.dot_general(a, b, (((1,), (1,)), ((), ())), preferred_element_type=F32)


def _dot_exact(a, b):
    return jnp.dot(a, b, preferred_element_type=F32, precision=PREC_EXACT)


def _rms(x, w):
    return x * lax.rsqrt(jnp.mean(x * x, axis=-1, keepdims=True) + EPS) * w


def _silu(x):
    return x * jax.nn.sigmoid(x)


def _softplus_neg_abs(x):
    return jnp.log1p(jnp.exp(-jnp.abs(x)))


def _row_tile(n_rows):
    return TOKEN_TILE if n_rows % TOKEN_TILE == 0 else n_rows


def _ffn_kernel(x_ref, nw_ref, wgu_ref, wd_ref, o_ref, *, d_ff, n_split):
    x = x_ref[...]
    h = _rms(x, nw_ref[...]).astype(BF16)
    fc = d_ff // n_split
    acc = jnp.zeros_like(x)
    for c in range(n_split):
        g = _dot(h, wgu_ref[:, c * fc:(c + 1) * fc])
        u = _dot(h, wgu_ref[:, d_ff + c * fc:d_ff + (c + 1) * fc])
        a = (_silu(g) * u).astype(BF16)
        acc = acc + _dot(a, wd_ref[c * fc:(c + 1) * fc, :])
    o_ref[...] = x + 0.5 * acc


def _ffn(x, norm_w, w_gu, w_down, layer):
    t, d = x.shape
    d_ff = w_down.shape[1]
    tm = _row_tile(t)
    n_split = 2 if (d_ff // 2) % 128 == 0 else 1
    of_layer = lambda i: (layer, 0, 0)
    return pl.pallas_call(
        functools.partial(_ffn_kernel, d_ff=d_ff, n_split=n_split),
        grid=(t // tm,),
        in_specs=[
            pl.BlockSpec((tm, d), lambda i: (i, 0)),
            pl.BlockSpec((None, 1, d), of_layer),
            pl.BlockSpec((None, d, 2 * d_ff), of_layer, pipeline_mode=pl.Buffered(1)),
            pl.BlockSpec((None, d_ff, d), of_layer, pipeline_mode=pl.Buffered(1)),
        ],
        out_specs=pl.BlockSpec((tm, d), lambda i: (i, 0)),
        out_shape=jax.ShapeDtypeStruct((t, d), F32),
        compiler_params=_params("parallel"),
        name="ffn",
    )(x, norm_w.reshape(-1, 1, d), w_gu, w_down)


def _normproj_kernel(x_ref, nw_ref, w_ref, o_ref):
    h = _rms(x_ref[...], nw_ref[...]).astype(BF16)
    o_ref[...] = _dot(h, w_ref[...])


def _normproj(x, norm_w, w):
    t, d = x.shape
    n = w.shape[1]
    tm = _row_tile(t)
    const = lambda i: (0, 0)
    return pl.pallas_call(
        _normproj_kernel,
        grid=(t // tm,),
        in_specs=[
            pl.BlockSpec((tm, d), lambda i: (i, 0)),
            pl.BlockSpec((1, d), const),
            pl.BlockSpec((d, n), const, pipeline_mode=pl.Buffered(1)),
        ],
        out_specs=pl.BlockSpec((tm, n), lambda i: (i, 0)),
        out_shape=jax.ShapeDtypeStruct((t, n), F32),
        compiler_params=_params("parallel"),
        name="gdn_in_proj",
    )(x, norm_w.reshape(1, d), w)


def _head_rms(y, w, h):
    seg = y[:, h * HEAD_DIM:(h + 1) * HEAD_DIM]
    return seg * lax.rsqrt(jnp.mean(seg * seg, axis=-1, keepdims=True) + EPS) * w


def _kv_kernel(x_ref, nw_ref, w_ref, kn_ref, k_ref, v_ref, kb_ref, vb_ref, *, n_heads):
    hid = _rms(x_ref[...], nw_ref[...]).astype(BF16)
    kv = _dot(hid, w_ref[...])
    d = n_heads * HEAD_DIM
    for h in range(n_heads):
        sl = slice(h * HEAD_DIM, (h + 1) * HEAD_DIM)
        k = _head_rms(kv, kn_ref[...], h)
        k_ref[:, sl] = k
        kb_ref[:, sl] = k.astype(BF16)
    v = kv[:, d:]
    v_ref[...] = v
    vb_ref[...] = v.astype(BF16)


def _shared_kv(x, kv_norm, w_kv, k_norm):
    t, d = x.shape
    n_heads = d // HEAD_DIM
    tm = _row_tile(t)
    const = lambda i: (0, 0)
    row = pl.BlockSpec((tm, d), lambda i: (i, 0))
    return pl.pallas_call(
        functools.partial(_kv_kernel, n_heads=n_heads),
        grid=(t // tm,),
        in_specs=[
            row,
            pl.BlockSpec((1, d), const),
            pl.BlockSpec((d, 2 * d), const, pipeline_mode=pl.Buffered(1)),
            pl.BlockSpec((1, HEAD_DIM), const),
        ],
        out_specs=[row, row, row, row],
        out_shape=[jax.ShapeDtypeStruct((t, d), F32), jax.ShapeDtypeStruct((t, d), F32),
                   jax.ShapeDtypeStruct((t, d), BF16), jax.ShapeDtypeStruct((t, d), BF16)],
        compiler_params=_params("parallel"),
        name="shared_kv",
    )(x, kv_norm.reshape(1, d), w_kv, k_norm.reshape(1, HEAD_DIM))


def _q_kernel(x_ref, nw_ref, w_ref, qn_ref, q_ref, *, n_heads):
    hid = _rms(x_ref[...], nw_ref[...]).astype(BF16)
    q = _dot(hid, w_ref[...])
    for h in range(n_heads):
        q_ref[:, h * HEAD_DIM:(h + 1) * HEAD_DIM] = (_head_rms(q, qn_ref[...], h) * Q_SCALE).astype(BF16)


def _sb_q(x, norm_w, w_q, q_norm):
    t, d = x.shape
    n_heads = d // HEAD_DIM
    tm = _row_tile(t)
    const = lambda i: (0, 0)
    row = pl.BlockSpec((tm, d), lambda i: (i, 0))
    return pl.pallas_call(
        functools.partial(_q_kernel, n_heads=n_heads),
        grid=(t // tm,),
        in_specs=[
            row,
            pl.BlockSpec((1, d), const),
            pl.BlockSpec((d, d), const, pipeline_mode=pl.Buffered(1)),
            pl.BlockSpec((1, HEAD_DIM), const),
        ],
        out_specs=row,
        out_shape=jax.ShapeDtypeStruct((t, d), BF16),
        compiler_params=_params("parallel"),
        name="sb_q_proj",
    )(x, norm_w.reshape(1, d), w_q, q_norm.reshape(1, HEAD_DIM))


def _outproj_kernel(x_ref, a_ref, w_ref, o_ref):
    o_ref[...] = x_ref[...] + _dot(a_ref[...], w_ref[...])


def _outproj(x, a, w):
    t, d = x.shape
    tm = _row_tile(t)
    row = pl.BlockSpec((tm, d), lambda i: (i, 0))
    return pl.pallas_call(
        _outproj_kernel,
        grid=(t // tm,),
        in_specs=[row, row, pl.BlockSpec((d, d), lambda i: (0, 0), pipeline_mode=pl.Buffered(1))],
        out_specs=row,
        out_shape=jax.ShapeDtypeStruct((t, d), F32),
        compiler_params=_params("parallel"),
        name="sb_out_proj",
    )(x, a, w)


def _unit_lower_inverses(ms, c, merge_masks):
    n_blk = c // SUBLANES
    col_id = lax.broadcasted_iota(jnp.int32, (SUBLANES, c), 1)
    row_id = lax.broadcasted_iota(jnp.int32, (SUBLANES, c), 0)
    m_blk = [[m[b * SUBLANES:(b + 1) * SUBLANES, :] for b in range(n_blk)] for m in ms]
    t_blk = [[(col_id == row_id + b * SUBLANES).astype(F32) for b in range(n_blk)] for _ in ms]
    for j in range(SUBLANES - 1):
        for i in range(len(ms)):
            for b in range(n_blk):
                col = m_blk[i][b][:, b * SUBLANES + j:b * SUBLANES + j + 1]
                t_blk[i][b] = t_blk[i][b] - col * t_blk[i][b][j:j + 1, :]
    ts = [jnp.concatenate(t, axis=0) if n_blk > 1 else t[0] for t in t_blk]
    for off_diag in merge_masks:
        cts = [_dot(jnp.where(off_diag, m, 0.0).astype(BF16), t.astype(BF16)) for m, t in zip(ms, ts)]
        ts = [t - _dot(t.astype(BF16), ct.astype(BF16)) for t, ct in zip(ts, cts)]
    return ts


def _merge_masks(c):
    ri = lax.broadcasted_iota(jnp.int32, (c, c), 0)
    ci = lax.broadcasted_iota(jnp.int32, (c, c), 1)
    masks = []
    shift = int(math.log2(SUBLANES))
    while (1 << shift) < c:
        same_pair = lax.shift_right_logical(ri, shift + 1) == lax.shift_right_logical(ci, shift + 1)
        same_blk = lax.shift_right_logical(ri, shift) == lax.shift_right_logical(ci, shift)
        masks.append(same_pair & jnp.logical_not(same_blk))
        shift += 1
    return masks


def _gdn_kernel(proj_ref, tail_ref, s0_ref, x_ref, cw_ref, alog_ref, dtb_ref, onorm_ref, wout_ref,
                xo_ref, sf_ref, state, xc, tail_s, og, *, n_heads, chunk, rows, valid_rows, n_blocks, n_seqs):
    qk_dim = n_heads * HEAD_DIM
    blk = pl.program_id(1)

    @pl.when(blk == 0)
    def _():
        state[...] = s0_ref[...]
        tail_s[...] = tail_ref[...]

    for sq in range(n_seqs):
        xc[sq, 0:SUBLANES, :] = tail_s[sq]
        xc[sq, SUBLANES:SUBLANES + rows, :] = proj_ref[sq * rows:(sq + 1) * rows, 0:3 * qk_dim]
        tail_s[sq] = xc[sq, rows:rows + SUBLANES, :]

    c = chunk
    ri = lax.broadcasted_iota(jnp.int32, (c, c), 0)
    ci = lax.broadcasted_iota(jnp.int32, (c, c), 1)
    lower_incl = (ri >= ci)
    lower_strict = (ri > ci)
    cum_mat = lower_incl.astype(F32)
    merge_masks = _merge_masks(c)
    neg_a =-jnp.exp(alog_ref[...])
    dtb = dtb_ref[...]
    onorm = onorm_ref[...]
    conv_taps = cw_ref.shape[0]

    def conv_seg(sq, r0, col0):
        win = xc[sq, pl.ds(r0, c + SUBLANES), col0:col0 + HEAD_DIM]
        acc = None
        for w in range(conv_taps):
            shift = conv_taps - 1 - w
            xw = win if shift == 0 else pltpu.roll(win, shift, axis=0)
            term = xw[SUBLANES:, :] * cw_ref[w:w + 1, col0:col0 + HEAD_DIM]
            acc = term if acc is None else acc + term
        return _silu(acc)

    def l2n(v):
        return v * lax.rsqrt(jnp.sum(v * v, axis=-1, keepdims=True) + EPS)

    def chunk_body(seqs, r0s):
        gates = {}
        for sq, ci in [(sq, ci) for ci in range(len(r0s)) for sq in seqs]:
            gate = proj_ref[pl.ds(sq * rows + r0s[ci], c), 4 * qk_dim:4 * qk_dim + HEAD_DIM]
            beta_all = jax.nn.sigmoid(gate)
            gpre = gate + dtb
            g_all = neg_a * (jnp.maximum(gpre, 0.0) + _softplus_neg_abs(gpre))
            if valid_rows < c:
                live = lax.broadcasted_iota(jnp.int32, (c, HEAD_DIM), 0) < valid_rows
                beta_all = jnp.where(live, beta_all, 0.0)
                g_all = jnp.where(live, g_all, 0.0)
            gc_all = _dot_exact(cum_mat, g_all)
            gates[sq, ci] = (beta_all, gc_all, gc_all.T)
        units = [(sq, h, ci) for ci in range(len(r0s)) for sq in seqs for h in range(n_heads)]
        q = [l2n(conv_seg(sq, r0s[ci], h * HEAD_DIM)) * (HEAD_DIM ** -0.5) for sq, h, ci in units]
        k = [l2n(conv_seg(sq, r0s[ci], qk_dim + h * HEAD_DIM)) for sq, h, ci in units]
        v = [conv_seg(sq, r0s[ci], 2 * qk_dim + h * HEAD_DIM) for sq, h, ci in units]
        lanes = lambda col: jnp.broadcast_to(col, (c, HEAD_DIM))
        beta = [lanes(gates[sq, ci][0][:, h:h + 1]) for sq, h, ci in units]
        gcol = [lanes(gates[sq, ci][1][:, n_heads + h:n_heads + h + 1]) for sq, h, ci in units]
        grow = [gates[sq, ci][2][n_heads + h:n_heads + h + 1, :] for sq, h, ci in units]
        n = range(len(units))
        decay = [jnp.exp(jnp.where(lower_incl, gcol[i][:, 0:c] - grow[i], -jnp.inf)) for i in n]
        kq = [jnp.concatenate([k[i], q[i]], axis=0).astype(BF16) for i in n]
        kk_qk = [_dot_nt(kq[i], kq[i][0:c, :]) for i in n]
        m = [jnp.where(lower_strict, beta[i][:, 0:c] * kk_qk[i][0:c, :] * decay[i], 0.0) for i in n]
        t_inv = _unit_lower_inverses(m, c, merge_masks)
        gam = [jnp.exp(gcol[i]) for i in n]
        rhs = [jnp.concatenate([(beta[i] * gam[i]) * k[i], beta[i] * v[i]], axis=1).astype(BF16) for i in n]
        wu = [_dot(t_inv[i].astype(BF16), rhs[i]) for i in n]
        g_last = [gcol[i][c - 1:c, :] for i in n]
        lhs2 = [jnp.concatenate([kk_qk[i][c:2 * c, :] * decay[i], (k[i] * jnp.exp(g_last[i] - gcol[i])).T],
                                axis=0).astype(BF16) for i in n]
        lhs1 = [jnp.concatenate([wu[i][:, 0:HEAD_DIM], q[i] * gam[i]], axis=0).astype(BF16) for i in n]
        for cj in range(len(r0s)):
            mine = [i for i in n if units[i][2] == cj]
            s = {i: state[units[i][0], units[i][1]] for i in mine}
            ws = {i: _dot(lhs1[i], s[i].astype(BF16)) for i in mine}
            ub = {i: (wu[i][:, HEAD_DIM:] - ws[i][0:c, :]).astype(BF16) for i in mine}
            upd = {i: _dot(lhs2[i], ub[i]) for i in mine}
            for i in mine:
                sq, h, _ = units[i]
                state[sq, h] = jnp.exp(g_last[i]) * s[i] + upd[i][c:, :]
                o = ws[i][c:2 * c, :] + upd[i][0:c, :]
                base = sq * rows + r0s[cj]
                z = proj_ref[pl.ds(base, c), 3 * qk_dim + h * HEAD_DIM:3 * qk_dim + (h + 1) * HEAD_DIM]
                og[pl.ds(base, c), h * HEAD_DIM:(h + 1) * HEAD_DIM] = _rms(o, onorm) * _silu(z)

    n_chunks = rows // c
    per_trip = GDN_CHUNKS_PER_TRIP if n_chunks % GDN_CHUNKS_PER_TRIP == 0 else 1
    if n_chunks == 1:
        chunk_body(list(range(n_seqs)), [0])
    else:
        def loop_body(trip, carry):
            r0 = pl.multiple_of(trip * (per_trip * c), per_trip * c)
            chunk_body(list(range(n_seqs)), [r0 + j * c for j in range(per_trip)])
            return carry
        lax.fori_loop(0, n_chunks // per_trip, loop_body, 0)
    xo_ref[...] = x_ref[...] + _dot(og[...].astype(BF16), wout_ref[...])

    @pl.when(blk == n_blocks - 1)
    def _():
        sf_ref[...] = state[...]


def _gdn_layer(x, proj, tail, s0, conv_w, a_log, dt_bias, o_norm, w_out, *, n_seq, seq_rows, chunk, valid_rows):
    t, d = x.shape
    n_heads = d // HEAD_DIM
    np_cols = proj.shape[1]
    rows = min(seq_rows, TOKEN_TILE)
    n_blocks = seq_rows // rows
    n_seqs = 1
    if n_blocks == 1 and n_seq % SAMPLE_SEQS_PER_STEP == 0:
        n_seqs = SAMPLE_SEQS_PER_STEP
    lane_pad = HEAD_DIM - 2 * n_heads
    alog_l = jnp.pad(a_log, (n_heads, lane_pad)).reshape(1, HEAD_DIM)
    dtb_l = jnp.pad(dt_bias, (n_heads, lane_pad)).reshape(1, HEAD_DIM)
    const2 = lambda b, r: (0, 0)
    row_map = lambda b, r: (b * n_blocks + r, 0)
    seq_map4 = lambda b, r: (b, 0, 0, 0)
    return pl.pallas_call(
        functools.partial(_gdn_kernel, n_heads=n_heads, chunk=chunk, rows=rows, valid_rows=valid_rows,
                          n_blocks=n_blocks, n_seqs=n_seqs),
        grid=(n_seq // n_seqs, n_blocks),
        in_specs=[
            pl.BlockSpec((n_seqs * rows, np_cols), row_map),
            pl.BlockSpec((n_seqs, SUBLANES, 3 * d), lambda b, r: (b, 0, 0)),
            pl.BlockSpec((n_seqs, n_heads, HEAD_DIM, HEAD_DIM), seq_map4),
            pl.BlockSpec((n_seqs * rows, d), row_map),
            pl.BlockSpec(conv_w.shape, const2),
            pl.BlockSpec((1, HEAD_DIM), const2),
            pl.BlockSpec((1, HEAD_DIM), const2),
            pl.BlockSpec((1, HEAD_DIM), const2),
            pl.BlockSpec((d, d), const2, pipeline_mode=pl.Buffered(1)),
        ],
        out_specs=[
            pl.BlockSpec((n_seqs * rows, d), row_map),
            pl.BlockSpec((n_seqs, n_heads, HEAD_DIM, HEAD_DIM), seq_map4),
        ],
        out_shape=[jax.ShapeDtypeStruct((t, d), F32),
                   jax.ShapeDtypeStruct((n_seq, n_heads, HEAD_DIM, HEAD_DIM), F32)],
        scratch_shapes=[
            pltpu.VMEM((n_seqs, n_heads, HEAD_DIM, HEAD_DIM), F32),
            pltpu.VMEM((n_seqs, rows + SUBLANES, 3 * d), F32),
            pltpu.VMEM((n_seqs, SUBLANES, 3 * d), F32),
            pltpu.VMEM((n_seqs * rows, d), F32),
        ],
        compiler_params=_params("arbitrary", "arbitrary"),
        name="gdn_mixer",
    )(proj, tail, s0, x, conv_w, alog_l, dtb_l, o_norm.reshape(1, HEAD_DIM), w_out)


def _sb_terms2(z):
    sp = jnp.log2(1.0 + jnp.exp2(-jnp.abs(z)))
    ls_pos = jnp.minimum(z, 0.0) - sp
    return ls_pos, ls_pos - z


def _split_bf16(x):
    hi = x.astype(BF16)
    lo = (x - hi.astype(F32)).astype(BF16)
    return hi, lo


def _sb_wave_kernel(qtab, ktab, bias_ref, q_ref, k_ref, v_ref, suf_ref, o_ref, lsp, lsn, rowsum, run, acc,
                    *, tile, n_qt, n_trips):
    head = pl.program_id(1)
    bias = bias_ref[head]
    suf = suf_ref[...]
    earlier = (lax.broadcasted_iota(jnp.int32, (tile, tile), 1) < lax.broadcasted_iota(jnp.int32, (tile, tile), 0))
    per_trip = ATT_TILES_PER_TRIP

    def terms(qi, kj, buf, slot, diagonal):
        q0 = pl.multiple_of(qi * tile, tile)
        k0 = pl.multiple_of(kj * tile, tile)
        z = _dot_nt(q_ref[pl.ds(q0, tile), :], k_ref[pl.ds(k0, tile), :]) + bias
        ls_pos, ls_neg = _sb_terms2(z)
        if diagonal:
            ls_neg = jnp.where(earlier, ls_neg, 0.0)
            ls_pos = jnp.where(earlier, ls_pos, -jnp.inf)
        lsp[buf, slot] = ls_pos
        lsn[buf, slot] = ls_neg.astype(BF16)
        rowsum[buf, slot] = jnp.sum(ls_neg, axis=-1, keepdims=True)

    def apply(si, kj, buf, slot, first):
        k0 = pl.multiple_of(kj * tile, tile)
        within = _dot(lsn[buf, slot], suf)
        if first:
            logw = lsp[buf, slot] + within
        else:
            logw = lsp[buf, slot] + (run[si] + within)
        wv = _dot(jnp.exp2(logw).astype(BF16), v_ref[pl.ds(k0, tile), :])
        if first:
            acc[si] = wv
            run[si] = rowsum[buf, slot]
        else:
            acc[si] += wv
            run[si] += rowsum[buf, slot]

    def diag_body(g, carry):
        for buf in range(2):
            for slot in range(per_trip):
                qi = (2 * g + buf) * per_trip + slot
                terms(qi, qi, buf, slot, True)
        for buf in range(2):
            for slot in range(per_trip):
                qi = (2 * g + buf) * per_trip + slot
                apply(qi, qi, buf, slot, True)
        return carry

    lax.fori_loop(0, n_qt // (2 * per_trip), diag_body, 0)
    run[n_qt] = jnp.zeros((tile, 1), F32)
    acc[n_qt] = jnp.zeros((tile, HEAD_DIM), F32)

    def trip_terms(t, buf):
        for slot in range(per_trip):
            f = t * per_trip + slot
            terms(jnp.minimum(qtab[f], n_qt - 1), ktab[f], buf, slot, False)

    def trip_apply(t, buf):
        for slot in range(per_trip):
            f = t * per_trip + slot
            apply(qtab[f], ktab[f], buf, slot, False)

    trip_terms(0, 0)

    def body(i, carry):
        t = 2 * i
        trip_terms(t + 1, 1)
        trip_apply(t, 0)
        trip_terms(t + 2, 0)
        trip_apply(t + 1, 1)
        return carry

    lax.fori_loop(0, n_trips // 2 - 1, body, 0)
    trip_terms(n_trips - 1, 1)
    trip_apply(n_trips - 2, 0)
    trip_apply(n_trips - 1, 1)
    o_ref[...] = acc[0:n_qt].reshape(n_qt * tile, HEAD_DIM).astype(o_ref.dtype)


def _sb_prompt_attention(q, k, v, logit_bias, *, n_seq, seq_len):
    t, d = q.shape
    n_heads = d // HEAD_DIM
    tile = ATT_TILE
    per_trip = ATT_TILES_PER_TRIP
    n_qt = seq_len // tile
    assert seq_len % tile == 0 and n_qt % (2 * per_trip) == 0
    tiles = [(qi, qi - dist) for dist in range(1, n_qt) for qi in range(dist, n_qt)]
    n_trips = max(2, 2 * -(-len(tiles) // (2 * per_trip)))
    tiles += [(n_qt, 0)] * (n_trips * per_trip - len(tiles))
    qtab = jnp.asarray([qi for qi, _ in tiles], jnp.int32)
    ktab = jnp.asarray([kj for _, kj in tiles], jnp.int32)
    suf = (jnp.arange(tile)[:, None] > jnp.arange(tile)[None, :]).astype(BF16)
    seq_spec = pl.BlockSpec((seq_len, HEAD_DIM), lambda b, h, qt, kt: (b, h))
    grid_spec = pltpu.PrefetchScalarGridSpec(
        num_scalar_prefetch=2,
        grid=(n_seq, n_heads),
        in_specs=[
            pl.BlockSpec(memory_space=pltpu.SMEM),
            seq_spec, seq_spec, seq_spec,
            pl.BlockSpec((tile, tile), lambda b, h, qt, kt: (0, 0)),
        ],
        out_specs=seq_spec,
        scratch_shapes=[
            pltpu.VMEM((2, per_trip, tile, tile), F32),
            pltpu.VMEM((2, per_trip, tile, tile), BF16),
            pltpu.VMEM((2, per_trip, tile, 1), F32),
            pltpu.VMEM((n_qt + 1, tile, 1), F32),
            pltpu.VMEM((n_qt + 1, tile, HEAD_DIM), F32),
        ],
    )
    return pl.pallas_call(
        functools.partial(_sb_wave_kernel, tile=tile, n_qt=n_qt, n_trips=n_trips),
        grid_spec=grid_spec,
        out_shape=jax.ShapeDtypeStruct((t, d), BF16),
        compiler_params=_params("parallel", "arbitrary"),
        name="sb_prompt_attention",
    )(qtab, ktab, logit_bias.astype(F32) * LOG2E, q, k, v, suf)


def _sb_sample_kernel(pt_ref, q_ref, kn_ref, vn_ref, *refs, n_heads, n_q, n_new, page, n_steps, per_step):
    del pt_ref
    kc_refs, vc_refs = refs[:per_step], refs[per_step:2 * per_step]
    bias_ref, scan_ref, o_ref, run, acc = refs[2 * per_step:]
    step = pl.program_id(1)
    heads = range(n_heads)
    scan = scan_ref[...]
    qrow = lax.broadcasted_iota(jnp.int32, (SUBLANES, HEAD_DIM), 0)
    kcol = lax.broadcasted_iota(jnp.int32, (SUBLANES, HEAD_DIM), 1)
    live_q = qrow < n_q
    q = [q_ref[0, h].astype(BF16) for h in heads]

    def log_weights(keys, live):
        units = [(g, h) for g in range(len(keys)) for h in heads]
        z = [_dot_nt(q[h], keys[g][h]) + bias_ref[h] for g, h in units]
        lt = [_sb_terms2(zz) for zz in z]
        ls_neg = [jnp.where(live, t[1], 0.0) for t in lt]
        rel, tot = {}, {}
        for g in range(len(keys)):
            hi, lo = _split_bf16(jnp.concatenate(ls_neg[g * n_heads:(g + 1) * n_heads], axis=0))
            res = _dot(hi, scan) + _dot(lo, scan)
            for h in heads:
                blk = res[h * SUBLANES:(h + 1) * SUBLANES, :]
                rel[g, h] = lt[g * n_heads + h][0] + blk[:, 0:HEAD_DIM]
                tot[g, h] = blk[:, HEAD_DIM:]
        return rel, tot

    def weighted_values(rel, run_val, live, values):
        w = jnp.exp2(jnp.where(live, rel + run_val, -jnp.inf)).astype(BF16)
        return _dot(w, values)

    def strided_head(ref, h):
        return ref[0, pl.ds(h, page, stride=n_heads), :].astype(BF16)

    @pl.when(step == 0)
    def _():
        live = live_q & (kcol < qrow) & (kcol < n_new)
        zpad = jnp.zeros((HEAD_DIM - SUBLANES, HEAD_DIM), F32)
        pad = lambda ref, h: jnp.concatenate([ref[0, h], zpad], axis=0).astype(BF16)
        rel, tot = log_weights([[pad(kn_ref, h) for h in heads]], live)
        for h in heads:
            acc[h] = weighted_values(rel[0, h], jnp.zeros((SUBLANES, HEAD_DIM), F32), live, pad(vn_ref, h))
            run[h] = tot[0, h]

    rel, tot = log_weights([[strided_head(kc_refs[g], h) for h in heads] for g in range(per_step)], live_q)
    for h in heads:
        run_val = run[h]
        acc_val = acc[h]
        for g in range(per_step):
            acc_val = acc_val + weighted_values(rel[g, h], run_val, live_q, strided_head(vc_refs[g], h))
            run_val = run_val + tot[g, h]
        run[h] = run_val
        acc[h] = acc_val

    @pl.when(step == n_steps - 1)
    def _():
        o_ref[0] = acc[...]


def _sb_sample_attention(q, k_new, v_new, cache_k, cache_v, page_table, logit_bias, *, n_seq, n_q):
    t, d = q.shape
    n_heads = d // HEAD_DIM
    n_phys, page = cache_k.shape[0], cache_k.shape[1]
    n_pages = page_table.shape[1]
    assert page == HEAD_DIM and n_q <= SUBLANES
    by_head = lambda a: jnp.pad(a.astype(F32).reshape(n_seq, n_q, n_heads, HEAD_DIM).transpose(0, 2, 1, 3),
                                ((0, 0), (0, 0), (0, SUBLANES - n_q), (0, 0)))
    kc = cache_k.reshape(n_phys, page * n_heads, HEAD_DIM)
    vc = cache_v.reshape(n_phys, page * n_heads, HEAD_DIM)
    key = jnp.arange(HEAD_DIM)
    scan = jnp.concatenate([key[:, None] > key[None, :], jnp.ones((HEAD_DIM, HEAD_DIM), bool)], axis=1).astype(BF16)
    per_step = math.gcd(n_pages, SAMPLE_PAGES_PER_STEP)
    n_steps = n_pages // per_step
    seq4 = lambda b, j, pt: (b, 0, 0, 0)
    head_spec = pl.BlockSpec((1, n_heads, SUBLANES, HEAD_DIM), seq4)
    page_specs = [pl.BlockSpec((1, page * n_heads, HEAD_DIM),
                               lambda b, j, pt, i=i: (pt[b, n_pages - 1 - (j * per_step + i)], 0, 0))
                  for i in range(per_step)]
    grid_spec = pltpu.PrefetchScalarGridSpec(
        num_scalar_prefetch=1,
        grid=(n_seq, n_steps),
        in_specs=[head_spec] * 3 + page_specs + page_specs + [
            pl.BlockSpec(memory_space=pltpu.SMEM),
            pl.BlockSpec((HEAD_DIM, 2 * HEAD_DIM), lambda b, j, pt: (0, 0)),
        ],
        out_specs=head_spec,
        scratch_shapes=[pltpu.VMEM((n_heads, SUBLANES, HEAD_DIM), F32)] * 2,
    )
    out = pl.pallas_call(
        functools.partial(_sb_sample_kernel, n_heads=n_heads, n_q=n_q, n_new=n_q, page=page, n_steps=n_steps,
                          per_step=per_step),
        grid_spec=grid_spec,
        out_shape=jax.ShapeDtypeStruct((n_seq, n_heads, SUBLANES, HEAD_DIM), F32),
        compiler_params=_params("parallel", "arbitrary"),
        name="sb_sample_attention",
    )(page_table, by_head(q), by_head(k_new), by_head(v_new), *([kc] * per_step), *([vc] * per_step),
      logit_bias.astype(F32) * LOG2E, scan)
    return out[:, :, :n_q].transpose(0, 2, 1, 3).reshape(t, d).astype(BF16)


def _trunk(x, tail, s0, past, w, *, n_seq, seq_len):
    d = x.shape[1]
    x = _ffn(x, w["ffn1_norm"], w["ffn1_w_gu"], w["ffn1_w_down"], 0)
    proj = _normproj(x, w["mix_norm"][0], w["gdn_w_in"])
    seq_rows = -(-seq_len // SUBLANES) * SUBLANES
    chunk = min(GDN_CHUNK, seq_rows)
    pad = lambda a: jnp.pad(a.reshape(n_seq, seq_len, -1), ((0, 0), (0, seq_rows - seq_len), (0, 0))
                            ).reshape(n_seq * seq_rows, -1)
    xg, pg = (x, proj) if seq_rows == seq_len else (pad(x), pad(proj))
    xg, state = _gdn_layer(xg, pg, tail, s0, w["gdn_conv_w"], w["gdn_a_log"], w["gdn_dt_bias"], w["gdn_o_norm"],
                           w["gdn_w_out"], n_seq=n_seq, seq_rows=seq_rows, chunk=chunk,
                           valid_rows=min(seq_len, chunk))
    x = xg if seq_rows == seq_len else xg.reshape(n_seq, seq_rows, d)[:, :seq_len].reshape(n_seq * seq_len, d)
    x = _ffn(x, w["ffn2_norm"], w["ffn2_w_gu"], w["ffn2_w_down"], 0)
    k, v, kb, vb = _shared_kv(x, w["kv_norm"], w["w_kv"], w["k_norm"])
    x = _ffn(x, w["ffn1_norm"], w["ffn1_w_gu"], w["ffn1_w_down"], 1)
    q = _sb_q(x, w["mix_norm"][1], w["sb_w_q"], w["sb_q_norm"])
    if past is None:
        att = _sb_prompt_attention(q, kb, vb, w["sb_logit_bias"], n_seq=n_seq, seq_len=seq_len)
    else:
        att = _sb_sample_attention(q, kb, vb, past[0], past[1], past[2], w["sb_logit_bias"], n_seq=n_seq, n_q=seq_len)
    x = _outproj(x, att, w["sb_w_out"])
    x = _ffn(x, w["ffn2_norm"], w["ffn2_w_gu"], w["ffn2_w_down"], 1)
    return x, proj.reshape(n_seq, seq_len, -1), state, k, v


def kernel(x_prompt, x_sample, state_gdn, state_conv, cache_k, cache_v, page_table, ffn1_norm, ffn1_w_gu,
           ffn1_w_down, ffn2_norm, ffn2_w_gu, ffn2_w_down, mix_norm, gdn_w_in, gdn_conv_w, gdn_a_log, gdn_dt_bias,
           gdn_o_norm, gdn_w_out, kv_norm, w_kv, k_norm, sb_w_q, sb_q_norm, sb_logit_bias, sb_w_out):
    bp, seq, d = x_prompt.shape
    bs, dec = x_sample.shape[0], x_sample.shape[1]
    n_heads = d // HEAD_DIM
    hist = state_conv.shape[2]
    assert gdn_w_in.shape[0] == 1 and sb_w_q.shape[0] == 1, "one layer of each mixer kind"
    assert dec <= SUBLANES and hist < SUBLANES and dec >= hist

    in_cols = gdn_w_in.shape[2]
    np_cols = -(-in_cols // HEAD_DIM) * HEAD_DIM
    w = {
        "ffn1_norm": ffn1_norm, "ffn2_norm": ffn2_norm, "mix_norm": mix_norm,
        "ffn1_w_gu": ffn1_w_gu.astype(BF16), "ffn1_w_down": ffn1_w_down.astype(BF16),
        "ffn2_w_gu": ffn2_w_gu.astype(BF16), "ffn2_w_down": ffn2_w_down.astype(BF16),
        "gdn_w_in": jnp.pad(gdn_w_in[0], ((0, 0), (0, np_cols - in_cols))).astype(BF16),
        "gdn_conv_w": gdn_conv_w[0], "gdn_a_log": gdn_a_log[0], "gdn_dt_bias": gdn_dt_bias[0],
        "gdn_o_norm": gdn_o_norm[0], "gdn_w_out": gdn_w_out[0].astype(BF16),
        "kv_norm": kv_norm, "w_kv": w_kv.astype(BF16), "k_norm": k_norm,
        "sb_w_q": sb_w_q[0].astype(BF16), "sb_q_norm": sb_q_norm[0], "sb_logit_bias": sb_logit_bias[0],
        "sb_w_out": sb_w_out[0].astype(BF16),
    }

    zero_tail = jnp.zeros((bp, SUBLANES, 3 * d), F32)
    zero_state = jnp.zeros((bp, n_heads, HEAD_DIM, HEAD_DIM), F32)
    y_p, qkv_p, st_p, k_p, v_p = _trunk(x_prompt.reshape(bp * seq, d), zero_tail, zero_state, None, w,
                                        n_seq=bp, seq_len=seq)

    tail_s = jnp.pad(state_conv[0], ((0, 0), (SUBLANES - hist, 0), (0, 0)))
    y_s, qkv_s, st_s, k_s, v_s = _trunk(x_sample.reshape(bs * dec, d), tail_s, state_gdn[0],
                                        (cache_k, cache_v, page_table), w, n_seq=bs, seq_len=dec)

    conv_p = qkv_p[:, seq - hist:seq, :3 * d][None]
    conv_s = qkv_s[:, dec - hist:dec, :3 * d][None]
    return (y_p.reshape(bp, seq, d), y_s.reshape(bs, dec, d), st_p[None].astype(x_prompt.dtype), conv_p,
            k_p.reshape(bp, seq, n_heads, HEAD_DIM), v_p.reshape(bp, seq, n_heads, HEAD_DIM),
            st_s[None].astype(x_sample.dtype), conv_s,
            k_s.reshape(bs, dec, n_heads, HEAD_DIM), v_s.reshape(bs, dec, n_heads, HEAD_DIM))
```

```python
import functools
import math

import jax
import jax.numpy as jnp
from jax import lax
from jax.experimental import pallas as pl
from jax.experimental.pallas import tpu as pltpu

F32 = jnp.float32
BF16 = jnp.bfloat16

EPS = 1e-6
HEAD_DIM = 128
GDN_CHUNK = 64
GDN_CHUNKS_PER_TRIP = 4
SUBLANES = 8
TOKEN_TILE = 512
ATT_TILE = 256
ATT_TILES_PER_TRIP = 4
SAMPLE_PAGES_PER_STEP = 8
SAMPLE_SEQS_PER_STEP = 4
VMEM_LIMIT = 56 * 1024 * 1024
PREC_EXACT = lax.Precision.HIGHEST
LOG2E = math.log2(math.e)
Q_SCALE = HEAD_DIM ** -0.5 * LOG2E


def _params(*sem):
    return pltpu.CompilerParams(dimension_semantics=sem, vmem_limit_bytes=VMEM_LIMIT)


def _dot(a, b):
    return jnp.dot(a, b, preferred_element_type=F32)


def _dot_nt(a, b):
    return lax.dot_general(a, b, (((1,), (1,)), ((), ())), preferred_element_type=F32)


def _dot_exact(a, b):
    return jnp.dot(a, b, preferred_element_type=F32, precision=PREC_EXACT)


def _rms(x, w):
    return x * lax.rsqrt(jnp.mean(x * x, axis=-1, keepdims=True) + EPS) * w


def _silu(x):
    return x * jax.nn.sigmoid(x)


def _softplus_neg_abs(x):
    return jnp.log1p(jnp.exp(-jnp.abs(x)))


def _row_tile(n_rows):
    return TOKEN_TILE if n_rows % TOKEN_TILE == 0 else n_rows


def _head_rms(y, w, h):
    seg = y[:, h * HEAD_DIM:(h + 1) * HEAD_DIM]
    return seg * lax.rsqrt(jnp.mean(seg * seg, axis=-1, keepdims=True) + EPS) * w


def _ffn_kernel(*refs, d_ff, n_split, mixer_out, q_heads):
    refs = list(refs)
    x_ref = refs.pop(0)
    a_ref, wo_ref = (refs.pop(0), refs.pop(0)) if mixer_out else (None, None)
    nw_ref, wgu_ref, wd_ref = refs.pop(0), refs.pop(0), refs.pop(0)
    mn_ref, wq_ref, qn_ref = (refs.pop(0), refs.pop(0), refs.pop(0)) if q_heads else (None, None, None)
    o_ref = refs.pop(0)
    x = x_ref[...]
    if mixer_out:
        x = x + _dot(a_ref[...], wo_ref[...])
    h = _rms(x, nw_ref[...]).astype(BF16)
    fc = d_ff // n_split
    acc = jnp.zeros_like(x)
    for c in range(n_split):
        g = _dot(h, wgu_ref[:, c * fc:(c + 1) * fc])
        u = _dot(h, wgu_ref[:, d_ff + c * fc:d_ff + (c + 1) * fc])
        a = (_silu(g) * u).astype(BF16)
        acc = acc + _dot(a, wd_ref[c * fc:(c + 1) * fc, :])
    y = x + 0.5 * acc
    o_ref[...] = y
    if q_heads:
        q_ref = refs.pop(0)
        q = _dot(_rms(y, mn_ref[...]).astype(BF16), wq_ref[...])
        for hd in range(q_heads):
            q_ref[:, hd * HEAD_DIM:(hd + 1) * HEAD_DIM] = (_head_rms(q, qn_ref[...], hd) * Q_SCALE).astype(BF16)


def _ffn(x, norm_w, w_gu, w_down, layer, mixer_out=None, q_proj=None):
    t, d = x.shape
    d_ff = w_down.shape[1]
    tm = _row_tile(t)
    n_split = 2 if (d_ff // 2) % 128 == 0 else 1
    of_layer = lambda i: (layer, 0, 0)
    const = lambda i: (0, 0)
    row = pl.BlockSpec((tm, d), lambda i: (i, 0))
    square = pl.BlockSpec((d, d), const, pipeline_mode=pl.Buffered(1))
    args, specs = [x], [row]
    if mixer_out is not None:
        args += list(mixer_out)
        specs += [row, square]
    args += [norm_w.reshape(-1, 1, d), w_gu, w_down]
    specs += [
        pl.BlockSpec((None, 1, d), of_layer),
        pl.BlockSpec((None, d, 2 * d_ff), of_layer, pipeline_mode=pl.Buffered(1)),
        pl.BlockSpec((None, d_ff, d), of_layer, pipeline_mode=pl.Buffered(1)),
    ]
    out_specs, out_shape = row, jax.ShapeDtypeStruct((t, d), F32)
    if q_proj is not None:
        args += [q_proj[0].reshape(1, d), q_proj[1], q_proj[2].reshape(1, HEAD_DIM)]
        specs += [pl.BlockSpec((1, d), const), square, pl.BlockSpec((1, HEAD_DIM), const)]
        out_specs, out_shape = [row, row], [out_shape, jax.ShapeDtypeStruct((t, d), BF16)]
    return pl.pallas_call(
        functools.partial(_ffn_kernel, d_ff=d_ff, n_split=n_split, mixer_out=mixer_out is not None,
                          q_heads=d // HEAD_DIM if q_proj is not None else 0),
        grid=(t // tm,),
        in_specs=specs,
        out_specs=out_specs,
        out_shape=out_shape,
        compiler_params=_params("parallel"),
        name="ffn",
    )(*args)


def _normproj_kernel(x_ref, nw_ref, w_ref, o_ref):
    h = _rms(x_ref[...], nw_ref[...]).astype(BF16)
    o_ref[...] = _dot(h, w_ref[...])


def _normproj(x, norm_w, w):
    t, d = x.shape
    n = w.shape[1]
    tm = _row_tile(t)
    const = lambda i: (0, 0)
    return pl.pallas_call(
        _normproj_kernel,
        grid=(t // tm,),
        in_specs=[
            pl.BlockSpec((tm, d), lambda i: (i, 0)),
            pl.BlockSpec((1, d), const),
            pl.BlockSpec((d, n), const, pipeline_mode=pl.Buffered(1)),
        ],
        out_specs=pl.BlockSpec((tm, n), lambda i: (i, 0)),
        out_shape=jax.ShapeDtypeStruct((t, n), F32),
        compiler_params=_params("parallel"),
        name="gdn_in_proj",
    )(x, norm_w.reshape(1, d), w)


def _kv_kernel(x_ref, nw_ref, w_ref, kn_ref, k_ref, v_ref, kb_ref, vb_ref, *, n_heads):
    hid = _rms(x_ref[...], nw_ref[...]).astype(BF16)
    kv = _dot(hid, w_ref[...])
    d = n_heads * HEAD_DIM
    for h in range(n_heads):
        sl = slice(h * HEAD_DIM, (h + 1) * HEAD_DIM)
        k = _head_rms(kv, kn_ref[...], h)
        k_ref[:, sl] = k
        kb_ref[:, sl] = k.astype(BF16)
    v = kv[:, d:]
    v_ref[...] = v
    vb_ref[...] = v.astype(BF16)


def _shared_kv(x, kv_norm, w_kv, k_norm):
    t, d = x.shape
    n_heads = d // HEAD_DIM
    tm = _row_tile(t)
    const = lambda i: (0, 0)
    row = pl.BlockSpec((tm, d), lambda i: (i, 0))
    return pl.pallas_call(
        functools.partial(_kv_kernel, n_heads=n_heads),
        grid=(t // tm,),
        in_specs=[
            row,
            pl.BlockSpec((1, d), const),
            pl.BlockSpec((d, 2 * d), const, pipeline_mode=pl.Buffered(1)),
            pl.BlockSpec((1, HEAD_DIM), const),
        ],
        out_specs=[row, row, row, row],
        out_shape=[jax.ShapeDtypeStruct((t, d), F32), jax.ShapeDtypeStruct((t, d), F32),
                   jax.ShapeDtypeStruct((t, d), BF16), jax.ShapeDtypeStruct((t, d), BF16)],
        compiler_params=_params("parallel"),
        name="shared_kv",
    )(x, kv_norm.reshape(1, d), w_kv, k_norm.reshape(1, HEAD_DIM))


def _unit_lower_inverses(ms, c, merge_masks):
    n_blk = c // SUBLANES
    col_id = lax.broadcasted_iota(jnp.int32, (SUBLANES, c), 1)
    row_id = lax.broadcasted_iota(jnp.int32, (SUBLANES, c), 0)
    m_blk = [[m[b * SUBLANES:(b + 1) * SUBLANES, :] for b in range(n_blk)] for m in ms]
    t_blk = [[(col_id == row_id + b * SUBLANES).astype(F32) for b in range(n_blk)] for _ in ms]
    for j in range(SUBLANES - 1):
        for i in range(len(ms)):
            for b in range(n_blk):
                col = m_blk[i][b][:, b * SUBLANES + j:b * SUBLANES + j + 1]
                t_blk[i][b] = t_blk[i][b] - col * t_blk[i][b][j:j + 1, :]
    ts = [jnp.concatenate(t, axis=0) if n_blk > 1 else t[0] for t in t_blk]
    for off_diag in merge_masks:
        cts = [_dot(jnp.where(off_diag, m, 0.0).astype(BF16), t.astype(BF16)) for m, t in zip(ms, ts)]
        ts = [t - _dot(t.astype(BF16), ct.astype(BF16)) for t, ct in zip(ts, cts)]
    return ts


def _merge_masks(c):
    ri = lax.broadcasted_iota(jnp.int32, (c, c), 0)
    ci = lax.broadcasted_iota(jnp.int32, (c, c), 1)
    masks = []
    shift = int(math.log2(SUBLANES))
    while (1 << shift) < c:
        same_pair = lax.shift_right_logical(ri, shift + 1) == lax.shift_right_logical(ci, shift + 1)
        same_blk = lax.shift_right_logical(ri, shift) == lax.shift_right_logical(ci, shift)
        masks.append(same_pair & jnp.logical_not(same_blk))
        shift += 1
    return masks


def _gdn_kernel(proj_ref, tail_ref, s0_ref, x_ref, cw_ref, alog_ref, dtb_ref, onorm_ref, wout_ref,
                xo_ref, sf_ref, state, xc, tail_s, og, *, n_heads, chunk, rows, valid_rows, n_blocks, n_seqs):
    qk_dim = n_heads * HEAD_DIM
    blk = pl.program_id(1)

    @pl.when(blk == 0)
    def _():
        state[...] = s0_ref[...]
        tail_s[...] = tail_ref[...]

    for sq in range(n_seqs):
        xc[sq, 0:SUBLANES, :] = tail_s[sq]
        xc[sq, SUBLANES:SUBLANES + rows, :] = proj_ref[sq * rows:(sq + 1) * rows, 0:3 * qk_dim]
        tail_s[sq] = xc[sq, rows:rows + SUBLANES, :]

    c = chunk
    ri = lax.broadcasted_iota(jnp.int32, (c, c), 0)
    ci = lax.broadcasted_iota(jnp.int32, (c, c), 1)
    lower_incl = (ri >= ci)
    lower_strict = (ri > ci)
    cum_mat = lower_incl.astype(F32)
    merge_masks = _merge_masks(c)
    neg_a =-jnp.exp(alog_ref[...])
    dtb = dtb_ref[...]
    onorm = onorm_ref[...]
    conv_taps = cw_ref.shape[0]

    def conv_seg(sq, r0, col0):
        win = xc[sq, pl.ds(r0, c + SUBLANES), col0:col0 + HEAD_DIM]
        acc = None
        for w in range(conv_taps):
            shift = conv_taps - 1 - w
            xw = win if shift == 0 else pltpu.roll(win, shift, axis=0)
            term = xw[SUBLANES:, :] * cw_ref[w:w + 1, col0:col0 + HEAD_DIM]
            acc = term if acc is None else acc + term
        return _silu(acc)

    def l2n(v):
        return v * lax.rsqrt(jnp.sum(v * v, axis=-1, keepdims=True) + EPS)

    def chunk_body(seqs, r0s):
        gates = {}
        for sq, ci in [(sq, ci) for ci in range(len(r0s)) for sq in seqs]:
            gate = proj_ref[pl.ds(sq * rows + r0s[ci], c), 4 * qk_dim:4 * qk_dim + HEAD_DIM]
            beta_all = jax.nn.sigmoid(gate)
            gpre = gate + dtb
            g_all = neg_a * (jnp.maximum(gpre, 0.0) + _softplus_neg_abs(gpre))
            if valid_rows < c:
                live = lax.broadcasted_iota(jnp.int32, (c, HEAD_DIM), 0) < valid_rows
                beta_all = jnp.where(live, beta_all, 0.0)
                g_all = jnp.where(live, g_all, 0.0)
            gc_all = _dot_exact(cum_mat, g_all)
            gates[sq, ci] = (beta_all, gc_all, gc_all.T)
        units = [(sq, h, ci) for ci in range(len(r0s)) for sq in seqs for h in range(n_heads)]
        q = [l2n(conv_seg(sq, r0s[ci], h * HEAD_DIM)) * (HEAD_DIM ** -0.5) for sq, h, ci in units]
        k = [l2n(conv_seg(sq, r0s[ci], qk_dim + h * HEAD_DIM)) for sq, h, ci in units]
        v = [conv_seg(sq, r0s[ci], 2 * qk_dim + h * HEAD_DIM) for sq, h, ci in units]
        lanes = lambda col: jnp.broadcast_to(col, (c, HEAD_DIM))
        beta = [lanes(gates[sq, ci][0][:, h:h + 1]) for sq, h, ci in units]
        gcol = [lanes(gates[sq, ci][1][:, n_heads + h:n_heads + h + 1]) for sq, h, ci in units]
        grow = [gates[sq, ci][2][n_heads + h:n_heads + h + 1, :] for sq, h, ci in units]
        n = range(len(units))
        decay = [jnp.exp(jnp.where(lower_incl, gcol[i][:, 0:c] - grow[i], -jnp.inf)) for i in n]
        kq = [jnp.concatenate([k[i], q[i]], axis=0).astype(BF16) for i in n]
        kk_qk = [_dot_nt(kq[i], kq[i][0:c, :]) for i in n]
        m = [jnp.where(lower_strict, beta[i][:, 0:c] * kk_qk[i][0:c, :] * decay[i], 0.0) for i in n]
        t_inv = _unit_lower_inverses(m, c, merge_masks)
        gam = [jnp.exp(gcol[i]) for i in n]
        rhs = [jnp.concatenate([(beta[i] * gam[i]) * k[i], beta[i] * v[i]], axis=1).astype(BF16) for i in n]
        wu = [_dot(t_inv[i].astype(BF16), rhs[i]) for i in n]
        g_last = [gcol[i][c - 1:c, :] for i in n]
        lhs2 = [jnp.concatenate([kk_qk[i][c:2 * c, :] * decay[i], (k[i] * jnp.exp(g_last[i] - gcol[i])).T],
                                axis=0).astype(BF16) for i in n]
        lhs1 = [jnp.concatenate([wu[i][:, 0:HEAD_DIM], q[i] * gam[i]], axis=0).astype(BF16) for i in n]
        for cj in range(len(r0s)):
            mine = [i for i in n if units[i][2] == cj]
            s = {i: state[units[i][0], units[i][1]] for i in mine}
            ws = {i: _dot(lhs1[i], s[i].astype(BF16)) for i in mine}
            ub = {i: (wu[i][:, HEAD_DIM:] - ws[i][0:c, :]).astype(BF16) for i in mine}
            upd = {i: _dot(lhs2[i], ub[i]) for i in mine}
            for i in mine:
                sq, h, _ = units[i]
                state[sq, h] = jnp.exp(g_last[i]) * s[i] + upd[i][c:, :]
                o = ws[i][c:2 * c, :] + upd[i][0:c, :]
                base = sq * rows + r0s[cj]
                z = proj_ref[pl.ds(base, c), 3 * qk_dim + h * HEAD_DIM:3 * qk_dim + (h + 1) * HEAD_DIM]
                og[pl.ds(base, c), h * HEAD_DIM:(h + 1) * HEAD_DIM] = _rms(o, onorm) * _silu(z)

    n_chunks = rows // c
    per_trip = GDN_CHUNKS_PER_TRIP if n_chunks % GDN_CHUNKS_PER_TRIP == 0 else 1
    if n_chunks == 1:
        chunk_body(list(range(n_seqs)), [0])
    else:
        def loop_body(trip, carry):
            r0 = pl.multiple_of(trip * (per_trip * c), per_trip * c)
            chunk_body(list(range(n_seqs)), [r0 + j * c for j in range(per_trip)])
            return carry
        lax.fori_loop(0, n_chunks // per_trip, loop_body, 0)
    xo_ref[...] = x_ref[...] + _dot(og[...].astype(BF16), wout_ref[...])

    @pl.when(blk == n_blocks - 1)
    def _():
        sf_ref[...] = state[...]


def _gdn_layer(x, proj, tail, s0, conv_w, a_log, dt_bias, o_norm, w_out, *, n_seq, seq_rows, chunk, valid_rows):
    t, d = x.shape
    n_heads = d // HEAD_DIM
    np_cols = proj.shape[1]
    rows = min(seq_rows, TOKEN_TILE)
    n_blocks = seq_rows // rows
    n_seqs = 1
    if n_blocks == 1 and n_seq % SAMPLE_SEQS_PER_STEP == 0:
        n_seqs = SAMPLE_SEQS_PER_STEP
    lane_pad = HEAD_DIM - 2 * n_heads
    alog_l = jnp.pad(a_log, (n_heads, lane_pad)).reshape(1, HEAD_DIM)
    dtb_l = jnp.pad(dt_bias, (n_heads, lane_pad)).reshape(1, HEAD_DIM)
    const2 = lambda b, r: (0, 0)
    row_map = lambda b, r: (b * n_blocks + r, 0)
    seq_map4 = lambda b, r: (b, 0, 0, 0)
    return pl.pallas_call(
        functools.partial(_gdn_kernel, n_heads=n_heads, chunk=chunk, rows=rows, valid_rows=valid_rows,
                          n_blocks=n_blocks, n_seqs=n_seqs),
        grid=(n_seq // n_seqs, n_blocks),
        in_specs=[
            pl.BlockSpec((n_seqs * rows, np_cols), row_map),
            pl.BlockSpec((n_seqs, SUBLANES, 3 * d), lambda b, r: (b, 0, 0)),
            pl.BlockSpec((n_seqs, n_heads, HEAD_DIM, HEAD_DIM), seq_map4),
            pl.BlockSpec((n_seqs * rows, d), row_map),
            pl.BlockSpec(conv_w.shape, const2),
            pl.BlockSpec((1, HEAD_DIM), const2),
            pl.BlockSpec((1, HEAD_DIM), const2),
            pl.BlockSpec((1, HEAD_DIM), const2),
            pl.BlockSpec((d, d), const2, pipeline_mode=pl.Buffered(1)),
        ],
        out_specs=[
            pl.BlockSpec((n_seqs * rows, d), row_map),
            pl.BlockSpec((n_seqs, n_heads, HEAD_DIM, HEAD_DIM), seq_map4),
        ],
        out_shape=[jax.ShapeDtypeStruct((t, d), F32),
                   jax.ShapeDtypeStruct((n_seq, n_heads, HEAD_DIM, HEAD_DIM), F32)],
        scratch_shapes=[
            pltpu.VMEM((n_seqs, n_heads, HEAD_DIM, HEAD_DIM), F32),
            pltpu.VMEM((n_seqs, rows + SUBLANES, 3 * d), F32),
            pltpu.VMEM((n_seqs, SUBLANES, 3 * d), F32),
            pltpu.VMEM((n_seqs * rows, d), F32),
        ],
        compiler_params=_params("arbitrary", "arbitrary"),
        name="gdn_mixer",
    )(proj, tail, s0, x, conv_w, alog_l, dtb_l, o_norm.reshape(1, HEAD_DIM), w_out)


def _sb_terms2(z):
    sp = jnp.log2(1.0 + jnp.exp2(-jnp.abs(z)))
    ls_pos = jnp.minimum(z, 0.0) - sp
    return ls_pos, ls_pos - z


def _split_bf16(x):
    hi = x.astype(BF16)
    lo = (x - hi.astype(F32)).astype(BF16)
    return hi, lo


def _sb_wave_kernel(qtab, ktab, bias_ref, q_ref, k_ref, v_ref, suf_ref, o_ref, lsp, lsn, rowsum, run, acc,
                    *, tile, n_qt, n_trips):
    head = pl.program_id(1)
    bias = bias_ref[head]
    suf = suf_ref[...]
    earlier = (lax.broadcasted_iota(jnp.int32, (tile, tile), 1) < lax.broadcasted_iota(jnp.int32, (tile, tile), 0))
    per_trip = ATT_TILES_PER_TRIP

    def terms(qi, kj, buf, slot, diagonal):
        q0 = pl.multiple_of(qi * tile, tile)
        k0 = pl.multiple_of(kj * tile, tile)
        z = _dot_nt(q_ref[pl.ds(q0, tile), :], k_ref[pl.ds(k0, tile), :]) + bias
        ls_pos, ls_neg = _sb_terms2(z)
        if diagonal:
            ls_neg = jnp.where(earlier, ls_neg, 0.0)
            ls_pos = jnp.where(earlier, ls_pos, -jnp.inf)
        lsp[buf, slot] = ls_pos
        lsn[buf, slot] = ls_neg.astype(BF16)
        rowsum[buf, slot] = jnp.sum(ls_neg, axis=-1, keepdims=True)

    def apply(si, kj, buf, slot, first):
        k0 = pl.multiple_of(kj * tile, tile)
        within = _dot(lsn[buf, slot], suf)
        if first:
            logw = lsp[buf, slot] + within
        else:
            logw = lsp[buf, slot] + (run[si] + within)
        wv = _dot(jnp.exp2(logw).astype(BF16), v_ref[pl.ds(k0, tile), :])
        if first:
            acc[si] = wv
            run[si] = rowsum[buf, slot]
        else:
            acc[si] += wv
            run[si] += rowsum[buf, slot]

    def diag_body(g, carry):
        for buf in range(2):
            for slot in range(per_trip):
                qi = (2 * g + buf) * per_trip + slot
                terms(qi, qi, buf, slot, True)
        for buf in range(2):
            for slot in range(per_trip):
                qi = (2 * g + buf) * per_trip + slot
                apply(qi, qi, buf, slot, True)
        return carry

    lax.fori_loop(0, n_qt // (2 * per_trip), diag_body, 0)
    run[n_qt] = jnp.zeros((tile, 1), F32)
    acc[n_qt] = jnp.zeros((tile, HEAD_DIM), F32)

    def trip_terms(t, buf):
        for slot in range(per_trip):
            f = t * per_trip + slot
            terms(jnp.minimum(qtab[f], n_qt - 1), ktab[f], buf, slot, False)

    def trip_apply(t, buf):
        for slot in range(per_trip):
            f = t * per_trip + slot
            apply(qtab[f], ktab[f], buf, slot, False)

    trip_terms(0, 0)

    def body(i, carry):
        t = 2 * i
        trip_terms(t + 1, 1)
        trip_apply(t, 0)
        trip_terms(t + 2, 0)
        trip_apply(t + 1, 1)
        return carry

    lax.fori_loop(0, n_trips // 2 - 1, body, 0)
    trip_terms(n_trips - 1, 1)
    trip_apply(n_trips - 2, 0)
    trip_apply(n_trips - 1, 1)
    o_ref[...] = acc[0:n_qt].reshape(n_qt * tile, HEAD_DIM).astype(o_ref.dtype)


def _sb_prompt_attention(q, k, v, logit_bias, *, n_seq, seq_len):
    t, d = q.shape
    n_heads = d // HEAD_DIM
    tile = ATT_TILE
    per_trip = ATT_TILES_PER_TRIP
    n_qt = seq_len // tile
    assert seq_len % tile == 0 and n_qt % (2 * per_trip) == 0
    tiles = [(qi, qi - dist) for dist in range(1, n_qt) for qi in range(dist, n_qt)]
    n_trips = max(2, 2 * -(-len(tiles) // (2 * per_trip)))
    tiles += [(n_qt, 0)] * (n_trips * per_trip - len(tiles))
    qtab = jnp.asarray([qi for qi, _ in tiles], jnp.int32)
    ktab = jnp.asarray([kj for _, kj in tiles], jnp.int32)
    suf = (jnp.arange(tile)[:, None] > jnp.arange(tile)[None, :]).astype(BF16)
    seq_spec = pl.BlockSpec((seq_len, HEAD_DIM), lambda b, h, qt, kt: (b, h))
    grid_spec = pltpu.PrefetchScalarGridSpec(
        num_scalar_prefetch=2,
        grid=(n_seq, n_heads),
        in_specs=[
            pl.BlockSpec(memory_space=pltpu.SMEM),
            seq_spec, seq_spec, seq_spec,
            pl.BlockSpec((tile, tile), lambda b, h, qt, kt: (0, 0)),
        ],
        out_specs=seq_spec,
        scratch_shapes=[
            pltpu.VMEM((2, per_trip, tile, tile), F32),
            pltpu.VMEM((2, per_trip, tile, tile), BF16),
            pltpu.VMEM((2, per_trip, tile, 1), F32),
            pltpu.VMEM((n_qt + 1, tile, 1), F32),
            pltpu.VMEM((n_qt + 1, tile, HEAD_DIM), F32),
        ],
    )
    return pl.pallas_call(
        functools.partial(_sb_wave_kernel, tile=tile, n_qt=n_qt, n_trips=n_trips),
        grid_spec=grid_spec,
        out_shape=jax.ShapeDtypeStruct((t, d), BF16),
        compiler_params=_params("parallel", "arbitrary"),
        name="sb_prompt_attention",
    )(qtab, ktab, logit_bias.astype(F32) * LOG2E, q, k, v, suf)


def _sb_sample_kernel(pt_ref, q_ref, kn_ref, vn_ref, *refs, n_heads, n_q, n_new, page, n_steps, per_step):
    del pt_ref
    kc_refs, vc_refs = refs[:per_step], refs[per_step:2 * per_step]
    bias_ref, scan_ref, o_ref, run, acc = refs[2 * per_step:]
    step = pl.program_id(1)
    heads = range(n_heads)
    scan = scan_ref[...]
    qrow = lax.broadcasted_iota(jnp.int32, (SUBLANES, HEAD_DIM), 0)
    kcol = lax.broadcasted_iota(jnp.int32, (SUBLANES, HEAD_DIM), 1)
    live_q = qrow < n_q
    q = [q_ref[0, h].astype(BF16) for h in heads]

    def log_weights(keys, live):
        units = [(g, h) for g in range(len(keys)) for h in heads]
        z = [_dot_nt(q[h], keys[g][h]) + bias_ref[h] for g, h in units]
        lt = [_sb_terms2(zz) for zz in z]
        ls_neg = [jnp.where(live, t[1], 0.0) for t in lt]
        rel, tot = {}, {}
        for g in range(len(keys)):
            hi, lo = _split_bf16(jnp.concatenate(ls_neg[g * n_heads:(g + 1) * n_heads], axis=0))
            res = _dot(hi, scan) + _dot(lo, scan)
            for h in heads:
                blk = res[h * SUBLANES:(h + 1) * SUBLANES, :]
                rel[g, h] = lt[g * n_heads + h][0] + blk[:, 0:HEAD_DIM]
                tot[g, h] = blk[:, HEAD_DIM:]
        return rel, tot

    def weighted_values(rel, run_val, live, values):
        w = jnp.exp2(jnp.where(live, rel + run_val, -jnp.inf)).astype(BF16)
        return _dot(w, values)

    def strided_head(ref, h):
        return ref[0, pl.ds(h, page, stride=n_heads), :].astype(BF16)

    @pl.when(step == 0)
    def _():
        live = live_q & (kcol < qrow) & (kcol < n_new)
        zpad = jnp.zeros((HEAD_DIM - SUBLANES, HEAD_DIM), F32)
        pad = lambda ref, h: jnp.concatenate([ref[0, h], zpad], axis=0).astype(BF16)
        rel, tot = log_weights([[pad(kn_ref, h) for h in heads]], live)
        for h in heads:
            acc[h] = weighted_values(rel[0, h], jnp.zeros((SUBLANES, HEAD_DIM), F32), live, pad(vn_ref, h))
            run[h] = tot[0, h]

    rel, tot = log_weights([[strided_head(kc_refs[g], h) for h in heads] for g in range(per_step)], live_q)
    for h in heads:
        run_val = run[h]
        acc_val = acc[h]
        for g in range(per_step):
            acc_val = acc_val + weighted_values(rel[g, h], run_val, live_q, strided_head(vc_refs[g], h))
            run_val = run_val + tot[g, h]
        run[h] = run_val
        acc[h] = acc_val

    @pl.when(step == n_steps - 1)
    def _():
        o_ref[0] = acc[...]


def _sb_sample_attention(q, k_new, v_new, cache_k, cache_v, page_table, logit_bias, *, n_seq, n_q):
    t, d = q.shape
    n_heads = d // HEAD_DIM
    n_phys, page = cache_k.shape[0], cache_k.shape[1]
    n_pages = page_table.shape[1]
    assert page == HEAD_DIM and n_q <= SUBLANES
    by_head = lambda a: jnp.pad(a.astype(F32).reshape(n_seq, n_q, n_heads, HEAD_DIM).transpose(0, 2, 1, 3),
                                ((0, 0), (0, 0), (0, SUBLANES - n_q), (0, 0)))
    kc = cache_k.reshape(n_phys, page * n_heads, HEAD_DIM)
    vc = cache_v.reshape(n_phys, page * n_heads, HEAD_DIM)
    key = jnp.arange(HEAD_DIM)
    scan = jnp.concatenate([key[:, None] > key[None, :], jnp.ones((HEAD_DIM, HEAD_DIM), bool)], axis=1).astype(BF16)
    per_step = math.gcd(n_pages, SAMPLE_PAGES_PER_STEP)
    n_steps = n_pages // per_step
    seq4 = lambda b, j, pt: (b, 0, 0, 0)
    head_spec = pl.BlockSpec((1, n_heads, SUBLANES, HEAD_DIM), seq4)
    page_specs = [pl.BlockSpec((1, page * n_heads, HEAD_DIM),
                               lambda b, j, pt, i=i: (pt[b, n_pages - 1 - (j * per_step + i)], 0, 0))
                  for i in range(per_step)]
    grid_spec = pltpu.PrefetchScalarGridSpec(
        num_scalar_prefetch=1,
        grid=(n_seq, n_steps),
        in_specs=[head_spec] * 3 + page_specs + page_specs + [
            pl.BlockSpec(memory_space=pltpu.SMEM),
            pl.BlockSpec((HEAD_DIM, 2 * HEAD_DIM), lambda b, j, pt: (0, 0)),
        ],
        out_specs=head_spec,
        scratch_shapes=[pltpu.VMEM((n_heads, SUBLANES, HEAD_DIM), F32)] * 2,
    )
    out = pl.pallas_call(
        functools.partial(_sb_sample_kernel, n_heads=n_heads, n_q=n_q, n_new=n_q, page=page, n_steps=n_steps,
                          per_step=per_step),
        grid_spec=grid_spec,
        out_shape=jax.ShapeDtypeStruct((n_seq, n_heads, SUBLANES, HEAD_DIM), F32),
        compiler_params=_params("parallel", "arbitrary"),
        name="sb_sample_attention",
    )(page_table, by_head(q), by_head(k_new), by_head(v_new), *([kc] * per_step), *([vc] * per_step),
      logit_bias.astype(F32) * LOG2E, scan)
    return out[:, :, :n_q].transpose(0, 2, 1, 3).reshape(t, d).astype(BF16)


def _trunk(x, tail, s0, past, w, *, n_seq, seq_len):
    d = x.shape[1]
    x = _ffn(x, w["ffn1_norm"], w["ffn1_w_gu"], w["ffn1_w_down"], 0)
    proj = _normproj(x, w["mix_norm"][0], w["gdn_w_in"])
    seq_rows = -(-seq_len // SUBLANES) * SUBLANES
    chunk = min(GDN_CHUNK, seq_rows)
    pad = lambda a: jnp.pad(a.reshape(n_seq, seq_len, -1), ((0, 0), (0, seq_rows - seq_len), (0, 0))
                            ).reshape(n_seq * seq_rows, -1)
    xg, pg = (x, proj) if seq_rows == seq_len else (pad(x), pad(proj))
    xg, state = _gdn_layer(xg, pg, tail, s0, w["gdn_conv_w"], w["gdn_a_log"], w["gdn_dt_bias"], w["gdn_o_norm"],
                           w["gdn_w_out"], n_seq=n_seq, seq_rows=seq_rows, chunk=chunk,
                           valid_rows=min(seq_len, chunk))
    x = xg if seq_rows == seq_len else xg.reshape(n_seq, seq_rows, d)[:, :seq_len].reshape(n_seq * seq_len, d)
    x = _ffn(x, w["ffn2_norm"], w["ffn2_w_gu"], w["ffn2_w_down"], 0)
    k, v, kb, vb = _shared_kv(x, w["kv_norm"], w["w_kv"], w["k_norm"])
    x, q = _ffn(x, w["ffn1_norm"], w["ffn1_w_gu"], w["ffn1_w_down"], 1,
                q_proj=(w["mix_norm"][1], w["sb_w_q"], w["sb_q_norm"]))
    if past is None:
        att = _sb_prompt_attention(q, kb, vb, w["sb_logit_bias"], n_seq=n_seq, seq_len=seq_len)
    else:
        att = _sb_sample_attention(q, kb, vb, past[0], past[1], past[2], w["sb_logit_bias"], n_seq=n_seq, n_q=seq_len)
    x = _ffn(x, w["ffn2_norm"], w["ffn2_w_gu"], w["ffn2_w_down"], 1, mixer_out=(att, w["sb_w_out"]))
    return x, proj.reshape(n_seq, seq_len, -1), state, k, v


def kernel(x_prompt, x_sample, state_gdn, state_conv, cache_k, cache_v, page_table, ffn1_norm, ffn1_w_gu,
           ffn1_w_down, ffn2_norm, ffn2_w_gu, ffn2_w_down, mix_norm, gdn_w_in, gdn_conv_w, gdn_a_log, gdn_dt_bias,
           gdn_o_norm, gdn_w_out, kv_norm, w_kv, k_norm, sb_w_q, sb_q_norm, sb_logit_bias, sb_w_out):
    bp, seq, d = x_prompt.shape
    bs, dec = x_sample.shape[0], x_sample.shape[1]
    n_heads = d // HEAD_DIM
    hist = state_conv.shape[2]
    assert gdn_w_in.shape[0] == 1 and sb_w_q.shape[0] == 1, "one layer of each mixer kind"
    assert dec <= SUBLANES and hist < SUBLANES and dec >= hist

    in_cols = gdn_w_in.shape[2]
    np_cols = -(-in_cols // HEAD_DIM) * HEAD_DIM
    w = {
        "ffn1_norm": ffn1_norm, "ffn2_norm": ffn2_norm, "mix_norm": mix_norm,
        "ffn1_w_gu": ffn1_w_gu.astype(BF16), "ffn1_w_down": ffn1_w_down.astype(BF16),
        "ffn2_w_gu": ffn2_w_gu.astype(BF16), "ffn2_w_down": ffn2_w_down.astype(BF16),
        "gdn_w_in": jnp.pad(gdn_w_in[0], ((0, 0), (0, np_cols - in_cols))).astype(BF16),
        "gdn_conv_w": gdn_conv_w[0], "gdn_a_log": gdn_a_log[0], "gdn_dt_bias": gdn_dt_bias[0],
        "gdn_o_norm": gdn_o_norm[0], "gdn_w_out": gdn_w_out[0].astype(BF16),
        "kv_norm": kv_norm, "w_kv": w_kv.astype(BF16), "k_norm": k_norm,
        "sb_w_q": sb_w_q[0].astype(BF16), "sb_q_norm": sb_q_norm[0], "sb_logit_bias": sb_logit_bias[0],
        "sb_w_out": sb_w_out[0].astype(BF16),
    }

    zero_tail = jnp.zeros((bp, SUBLANES, 3 * d), F32)
    zero_state = jnp.zeros((bp, n_heads, HEAD_DIM, HEAD_DIM), F32)
    y_p, qkv_p, st_p, k_p, v_p = _trunk(x_prompt.reshape(bp * seq, d), zero_tail, zero_state, None, w,
                                        n_seq=bp, seq_len=seq)

    tail_s = jnp.pad(state_conv[0], ((0, 0), (SUBLANES - hist, 0), (0, 0)))
    y_s, qkv_s, st_s, k_s, v_s = _trunk(x_sample.reshape(bs * dec, d), tail_s, state_gdn[0],
                                        (cache_k, cache_v, page_table), w, n_seq=bs, seq_len=dec)

    conv_p = qkv_p[:, seq - hist:seq, :3 * d][None]
    conv_s = qkv_s[:, dec - hist:dec, :3 * d][None]
    return (y_p.reshape(bp, seq, d), y_s.reshape(bs, dec, d), st_p[None].astype(x_prompt.dtype), conv_p,
            k_p.reshape(bp, seq, n_heads, HEAD_DIM), v_p.reshape(bp, seq, n_heads, HEAD_DIM),
            st_s[None].astype(x_sample.dtype), conv_s,
            k_s.reshape(bs, dec, n_heads, HEAD_DIM), v_s.reshape(bs, dec, n_heads, HEAD_DIM))
```

```python
import functools
import math

import jax
import jax.numpy as jnp
from jax import lax
from jax.experimental import pallas as pl
from jax.experimental.pallas import tpu as pltpu

F32 = jnp.float32
BF16 = jnp.bfloat16

EPS = 1e-6
HEAD_DIM = 128
GDN_CHUNK = 64
GDN_CHUNKS_PER_TRIP = 4
SUBLANES = 8
TOKEN_TILE = 512
FFN_ROW_TILE = 1024
FFN_CHUNK = 256
ATT_TILE = 256
ATT_TILES_PER_TRIP = 4
SAMPLE_PAGES_PER_STEP = 8
SAMPLE_SEQS_PER_STEP = 4
VMEM_LIMIT = 56 * 1024 * 1024
PREC_EXACT = lax.Precision.HIGHEST
LOG2E = math.log2(math.e)
Q_SCALE = HEAD_DIM ** -0.5 * LOG2E


def _params(*sem):
    return pltpu.CompilerParams(dimension_semantics=sem, vmem_limit_bytes=VMEM_LIMIT)


def _dot(a, b):
    return jnp.dot(a, b, preferred_element_type=F32)


def _dot_nt(a, b):
    return lax.dot_general(a, b, (((1,), (1,)), ((), ())), preferred_element_type=F32)


def _dot_exact(a, b):
    return jnp.dot(a, b, preferred_element_type=F32, precision=PREC_EXACT)


def _rms(x, w):
    return x * lax.rsqrt(jnp.mean(x * x, axis=-1, keepdims=True) + EPS) * w


def _silu(x):
    return x * jax.nn.sigmoid(x)


def _softplus_neg_abs(x):
    return jnp.log1p(jnp.exp(-jnp.abs(x)))


def _row_tile(n_rows):
    return TOKEN_TILE if n_rows % TOKEN_TILE == 0 else n_rows


def _head_rms(y, w, h):
    seg = y[:, h * HEAD_DIM:(h + 1) * HEAD_DIM]
    return seg * lax.rsqrt(jnp.mean(seg * seg, axis=-1, keepdims=True) + EPS) * w


def _ffn_kernel(*refs, d_ff, n_split, mixer_out, q_heads):
    refs = list(refs)
    x_ref = refs.pop(0)
    a_ref, wo_ref = (refs.pop(0), refs.pop(0)) if mixer_out else (None, None)
    nw_ref, wgu_ref, wd_ref = refs.pop(0), refs.pop(0), refs.pop(0)
    mn_ref, wq_ref, qn_ref = (refs.pop(0), refs.pop(0), refs.pop(0)) if q_heads else (None, None, None)
    o_ref = refs.pop(0)
    x = x_ref[...]
    if mixer_out:
        x = x + _dot(a_ref[...], wo_ref[...])
    h = _rms(x, nw_ref[...]).astype(BF16)
    fc = d_ff // n_split
    acc = jnp.zeros_like(x)
    for c in range(n_split):
        g = _dot(h, wgu_ref[:, c * fc:(c + 1) * fc])
        u = _dot(h, wgu_ref[:, d_ff + c * fc:d_ff + (c + 1) * fc])
        a = (_silu(g) * u).astype(BF16)
        acc = acc + _dot(a, wd_ref[c * fc:(c + 1) * fc, :])
    y = x + 0.5 * acc
    o_ref[...] = y
    if q_heads:
        q_ref = refs.pop(0)
        q = _dot(_rms(y, mn_ref[...]).astype(BF16), wq_ref[...])
        for hd in range(q_heads):
            q_ref[:, hd * HEAD_DIM:(hd + 1) * HEAD_DIM] = (_head_rms(q, qn_ref[...], hd) * Q_SCALE).astype(BF16)


def _ffn(x, norm_w, w_gu, w_down, layer, mixer_out=None, q_proj=None):
    t, d = x.shape
    d_ff = w_down.shape[1]
    tm = FFN_ROW_TILE if t % FFN_ROW_TILE == 0 else _row_tile(t)
    n_split = d_ff // FFN_CHUNK if d_ff % FFN_CHUNK == 0 else 1
    of_layer = lambda i: (layer, 0, 0)
    const = lambda i: (0, 0)
    row = pl.BlockSpec((tm, d), lambda i: (i, 0))
    square =pl.BlockSpec((d, d), const, pipeline_mode=pl.Buffered(1))
    args, specs = [x], [row]
    if mixer_out is not None:
        args += list(mixer_out)
        specs += [row, square]
    args += [norm_w.reshape(-1, 1, d), w_gu, w_down]
    specs += [
        pl.BlockSpec((None, 1, d), of_layer),
        pl.BlockSpec((None, d, 2 * d_ff), of_layer, pipeline_mode=pl.Buffered(1)),
        pl.BlockSpec((None, d_ff, d), of_layer, pipeline_mode=pl.Buffered(1)),
    ]
    out_specs, out_shape = row, jax.ShapeDtypeStruct((t, d), F32)
    if q_proj is not None:
        args += [q_proj[0].reshape(1, d), q_proj[1], q_proj[2].reshape(1, HEAD_DIM)]
        specs += [pl.BlockSpec((1, d), const), square, pl.BlockSpec((1, HEAD_DIM), const)]
        out_specs, out_shape = [row, row], [out_shape, jax.ShapeDtypeStruct((t, d), BF16)]
    return pl.pallas_call(
        functools.partial(_ffn_kernel, d_ff=d_ff, n_split=n_split, mixer_out=mixer_out is not None,
                          q_heads=d // HEAD_DIM if q_proj is not None else 0),
        grid=(t // tm,),
        in_specs=specs,
        out_specs=out_specs,
        out_shape=out_shape,
        compiler_params=_params("parallel"),
        name="ffn",
    )(*args)


def _normproj_kernel(x_ref, nw_ref, w_ref, o_ref):
    h = _rms(x_ref[...], nw_ref[...]).astype(BF16)
    o_ref[...] = _dot(h, w_ref[...])


def _normproj(x, norm_w, w):
    t, d = x.shape
    n = w.shape[1]
    tm = _row_tile(t)
    const = lambda i: (0, 0)
    return pl.pallas_call(
        _normproj_kernel,
        grid=(t // tm,),
        in_specs=[
            pl.BlockSpec((tm, d), lambda i: (i, 0)),
            pl.BlockSpec((1, d), const),
            pl.BlockSpec((d, n), const, pipeline_mode=pl.Buffered(1)),
        ],
        out_specs=pl.BlockSpec((tm, n), lambda i: (i, 0)),
        out_shape=jax.ShapeDtypeStruct((t, n), F32),
        compiler_params=_params("parallel"),
        name="gdn_in_proj",
    )(x, norm_w.reshape(1, d), w)


def _kv_kernel(x_ref, nw_ref, w_ref, kn_ref, k_ref, v_ref, kb_ref, vb_ref, *, n_heads):
    hid = _rms(x_ref[...], nw_ref[...]).astype(BF16)
    kv = _dot(hid, w_ref[...])
    d = n_heads * HEAD_DIM
    for h in range(n_heads):
        sl = slice(h * HEAD_DIM, (h + 1) * HEAD_DIM)
        k = _head_rms(kv, kn_ref[...], h)
        k_ref[:, sl] = k
        kb_ref[:, sl] = k.astype(BF16)
    v = kv[:, d:]
    v_ref[...] = v
    vb_ref[...] = v.astype(BF16)


def _shared_kv(x, kv_norm, w_kv, k_norm):
    t, d = x.shape
    n_heads = d // HEAD_DIM
    tm = _row_tile(t)
    const = lambda i: (0, 0)
    row = pl.BlockSpec((tm, d), lambda i: (i, 0))
    return pl.pallas_call(
        functools.partial(_kv_kernel, n_heads=n_heads),
        grid=(t // tm,),
        in_specs=[
            row,
            pl.BlockSpec((1, d), const),
            pl.BlockSpec((d, 2 * d), const, pipeline_mode=pl.Buffered(1)),
            pl.BlockSpec((1, HEAD_DIM), const),
        ],
        out_specs=[row, row, row, row],
        out_shape=[jax.ShapeDtypeStruct((t, d), F32), jax.ShapeDtypeStruct((t, d), F32),
                   jax.ShapeDtypeStruct((t, d), BF16), jax.ShapeDtypeStruct((t, d), BF16)],
        compiler_params=_params("parallel"),
        name="shared_kv",
    )(x, kv_norm.reshape(1, d), w_kv, k_norm.reshape(1, HEAD_DIM))


def _unit_lower_inverses(ms, c, merge_masks):
    n_blk = c // SUBLANES
    col_id = lax.broadcasted_iota(jnp.int32, (SUBLANES, c), 1)
    row_id = lax.broadcasted_iota(jnp.int32, (SUBLANES, c), 0)
    m_blk = [[m[b * SUBLANES:(b + 1) * SUBLANES, :] for b in range(n_blk)] for m in ms]
    t_blk = [[(col_id == row_id + b * SUBLANES).astype(F32) for b in range(n_blk)] for _ in ms]
    for j in range(SUBLANES - 1):
        for i in range(len(ms)):
            for b in range(n_blk):
                col = m_blk[i][b][:, b * SUBLANES + j:b * SUBLANES + j + 1]
                t_blk[i][b] = t_blk[i][b] - col * t_blk[i][b][j:j + 1, :]
    ts = [jnp.concatenate(t, axis=0) if n_blk > 1 else t[0] for t in t_blk]
    for off_diag in merge_masks:
        cts = [_dot(jnp.where(off_diag, m, 0.0).astype(BF16), t.astype(BF16)) for m, t in zip(ms, ts)]
        ts = [t - _dot(t.astype(BF16), ct.astype(BF16)) for t, ct in zip(ts, cts)]
    return ts


def _merge_masks(c):
    ri = lax.broadcasted_iota(jnp.int32, (c, c), 0)
    ci = lax.broadcasted_iota(jnp.int32, (c, c), 1)
    masks = []
    shift = int(math.log2(SUBLANES))
    while (1 << shift) < c:
        same_pair = lax.shift_right_logical(ri, shift + 1) == lax.shift_right_logical(ci, shift + 1)
        same_blk = lax.shift_right_logical(ri, shift) == lax.shift_right_logical(ci, shift)
        masks.append(same_pair & jnp.logical_not(same_blk))
        shift += 1
    return masks


def _gdn_kernel(proj_ref, tail_ref, s0_ref, x_ref, cw_ref, alog_ref, dtb_ref, onorm_ref, wout_ref,
                xo_ref, sf_ref, state, xc, tail_s, og, *, n_heads, chunk, rows, valid_rows, n_blocks, n_seqs):
    qk_dim = n_heads * HEAD_DIM
    blk = pl.program_id(1)

    @pl.when(blk == 0)
    def _():
        state[...] = s0_ref[...]
        tail_s[...] = tail_ref[...]

    for sq in range(n_seqs):
        xc[sq, 0:SUBLANES, :] = tail_s[sq]
        xc[sq, SUBLANES:SUBLANES + rows, :] = proj_ref[sq * rows:(sq + 1) * rows, 0:3 * qk_dim]
        tail_s[sq] = xc[sq, rows:rows + SUBLANES, :]

    c = chunk
    ri = lax.broadcasted_iota(jnp.int32, (c, c), 0)
    ci = lax.broadcasted_iota(jnp.int32, (c, c), 1)
    lower_incl = (ri >= ci)
    lower_strict = (ri > ci)
    cum_mat = lower_incl.astype(F32)
    merge_masks = _merge_masks(c)
    neg_a =-jnp.exp(alog_ref[...])
    dtb = dtb_ref[...]
    onorm = onorm_ref[...]
    conv_taps = cw_ref.shape[0]

    def conv_seg(sq, r0, col0):
        win = xc[sq, pl.ds(r0, c + SUBLANES), col0:col0 + HEAD_DIM]
        acc = None
        for w in range(conv_taps):
            shift = conv_taps - 1 - w
            xw = win if shift == 0 else pltpu.roll(win, shift, axis=0)
            term = xw[SUBLANES:, :] * cw_ref[w:w + 1, col0:col0 + HEAD_DIM]
            acc = term if acc is None else acc + term
        return _silu(acc)

    def l2n(v):
        return v * lax.rsqrt(jnp.sum(v * v, axis=-1, keepdims=True) + EPS)

    def chunk_body(seqs, r0s):
        gates = {}
        for sq, ci in [(sq, ci) for ci in range(len(r0s)) for sq in seqs]:
            gate = proj_ref[pl.ds(sq * rows + r0s[ci], c), 4 * qk_dim:4 * qk_dim + HEAD_DIM]
            beta_all = jax.nn.sigmoid(gate)
            gpre = gate + dtb
            g_all = neg_a * (jnp.maximum(gpre, 0.0) + _softplus_neg_abs(gpre))
            if valid_rows < c:
                live = lax.broadcasted_iota(jnp.int32, (c, HEAD_DIM), 0) < valid_rows
                beta_all = jnp.where(live, beta_all, 0.0)
                g_all = jnp.where(live, g_all, 0.0)
            gc_all = _dot_exact(cum_mat, g_all)
            gates[sq, ci] = (beta_all, gc_all, gc_all.T)
        units = [(sq, h, ci) for ci in range(len(r0s)) for sq in seqs for h in range(n_heads)]
        q = [l2n(conv_seg(sq, r0s[ci], h * HEAD_DIM)) * (HEAD_DIM ** -0.5) for sq, h, ci in units]
        k = [l2n(conv_seg(sq, r0s[ci], qk_dim + h * HEAD_DIM)) for sq, h, ci in units]
        v = [conv_seg(sq, r0s[ci], 2 * qk_dim + h * HEAD_DIM) for sq, h, ci in units]
        lanes = lambda col: jnp.broadcast_to(col, (c, HEAD_DIM))
        beta = [lanes(gates[sq, ci][0][:, h:h + 1]) for sq, h, ci in units]
        gcol = [lanes(gates[sq, ci][1][:, n_heads + h:n_heads + h + 1]) for sq, h, ci in units]
        grow = [gates[sq, ci][2][n_heads + h:n_heads + h + 1, :] for sq, h, ci in units]
        n = range(len(units))
        decay = [jnp.exp(jnp.where(lower_incl, gcol[i][:, 0:c] - grow[i], -jnp.inf)) for i in n]
        kq = [jnp.concatenate([k[i], q[i]], axis=0).astype(BF16) for i in n]
        kk_qk = [_dot_nt(kq[i], kq[i][0:c, :]) for i in n]
        m = [jnp.where(lower_strict, beta[i][:, 0:c] * kk_qk[i][0:c, :] * decay[i], 0.0) for i in n]
        t_inv = _unit_lower_inverses(m, c, merge_masks)
        gam = [jnp.exp(gcol[i]) for i in n]
        rhs = [jnp.concatenate([(beta[i] * gam[i]) * k[i], beta[i] * v[i]], axis=1).astype(BF16) for i in n]
        wu = [_dot(t_inv[i].astype(BF16), rhs[i]) for i in n]
        g_last = [gcol[i][c - 1:c, :] for i in n]
        lhs2 = [jnp.concatenate([kk_qk[i][c:2 * c, :] * decay[i], (k[i] * jnp.exp(g_last[i] - gcol[i])).T],
                                axis=0).astype(BF16) for i in n]
        lhs1 = [jnp.concatenate([wu[i][:, 0:HEAD_DIM], q[i] * gam[i]], axis=0).astype(BF16) for i in n]
        for cj in range(len(r0s)):
            mine = [i for i in n if units[i][2] == cj]
            s = {i: state[units[i][0], units[i][1]] for i in mine}
            ws = {i: _dot(lhs1[i], s[i].astype(BF16)) for i in mine}
            ub = {i: (wu[i][:, HEAD_DIM:] - ws[i][0:c, :]).astype(BF16) for i in mine}
            upd = {i: _dot(lhs2[i], ub[i]) for i in mine}
            for i in mine:
                sq, h, _ = units[i]
                state[sq, h] = jnp.exp(g_last[i]) * s[i] + upd[i][c:, :]
                o = ws[i][c:2 * c, :] + upd[i][0:c, :]
                base = sq * rows + r0s[cj]
                z = proj_ref[pl.ds(base, c), 3 * qk_dim + h * HEAD_DIM:3 * qk_dim + (h + 1) * HEAD_DIM]
                og[pl.ds(base, c), h * HEAD_DIM:(h + 1) * HEAD_DIM] = _rms(o, onorm) * _silu(z)

    n_chunks = rows // c
    per_trip = GDN_CHUNKS_PER_TRIP if n_chunks % GDN_CHUNKS_PER_TRIP == 0 else 1
    if n_chunks == 1:
        chunk_body(list(range(n_seqs)), [0])
    else:
        def loop_body(trip, carry):
            r0 = pl.multiple_of(trip * (per_trip * c), per_trip * c)
            chunk_body(list(range(n_seqs)), [r0 + j * c for j in range(per_trip)])
            return carry
        lax.fori_loop(0, n_chunks // per_trip, loop_body, 0)
    xo_ref[...] = x_ref[...] + _dot(og[...].astype(BF16), wout_ref[...])

    @pl.when(blk == n_blocks - 1)
    def _():
        sf_ref[...] = state[...]


def _gdn_layer(x, proj, tail, s0, conv_w, a_log, dt_bias, o_norm, w_out, *, n_seq, seq_rows, chunk, valid_rows):
    t, d = x.shape
    n_heads = d // HEAD_DIM
    np_cols = proj.shape[1]
    rows = min(seq_rows, TOKEN_TILE)
    n_blocks = seq_rows // rows
    n_seqs = 1
    if n_blocks == 1 and n_seq % SAMPLE_SEQS_PER_STEP == 0:
        n_seqs = SAMPLE_SEQS_PER_STEP
    lane_pad = HEAD_DIM - 2 * n_heads
    alog_l = jnp.pad(a_log, (n_heads, lane_pad)).reshape(1, HEAD_DIM)
    dtb_l = jnp.pad(dt_bias, (n_heads, lane_pad)).reshape(1, HEAD_DIM)
    const2 = lambda b, r: (0, 0)
    row_map = lambda b, r: (b * n_blocks + r, 0)
    seq_map4 = lambda b, r: (b, 0, 0, 0)
    return pl.pallas_call(
        functools.partial(_gdn_kernel, n_heads=n_heads, chunk=chunk, rows=rows, valid_rows=valid_rows,
                          n_blocks=n_blocks, n_seqs=n_seqs),
        grid=(n_seq // n_seqs, n_blocks),
        in_specs=[
            pl.BlockSpec((n_seqs * rows, np_cols), row_map),
            pl.BlockSpec((n_seqs, SUBLANES, 3 * d), lambda b, r: (b, 0, 0)),
            pl.BlockSpec((n_seqs, n_heads, HEAD_DIM, HEAD_DIM), seq_map4),
            pl.BlockSpec((n_seqs * rows, d), row_map),
            pl.BlockSpec(conv_w.shape, const2),
            pl.BlockSpec((1, HEAD_DIM), const2),
            pl.BlockSpec((1, HEAD_DIM), const2),
            pl.BlockSpec((1, HEAD_DIM), const2),
            pl.BlockSpec((d, d), const2, pipeline_mode=pl.Buffered(1)),
        ],
        out_specs=[
            pl.BlockSpec((n_seqs * rows, d), row_map),
            pl.BlockSpec((n_seqs, n_heads, HEAD_DIM, HEAD_DIM), seq_map4),
        ],
        out_shape=[jax.ShapeDtypeStruct((t, d), F32),
                   jax.ShapeDtypeStruct((n_seq, n_heads, HEAD_DIM, HEAD_DIM), F32)],
        scratch_shapes=[
            pltpu.VMEM((n_seqs, n_heads, HEAD_DIM, HEAD_DIM), F32),
            pltpu.VMEM((n_seqs, rows + SUBLANES, 3 * d), F32),
            pltpu.VMEM((n_seqs, SUBLANES, 3 * d), F32),
            pltpu.VMEM((n_seqs * rows, d), F32),
        ],
        compiler_params=_params("arbitrary", "arbitrary"),
        name="gdn_mixer",
    )(proj, tail, s0, x, conv_w, alog_l, dtb_l, o_norm.reshape(1, HEAD_DIM), w_out)


def _sb_terms2(z):
    sp = jnp.log2(1.0 + jnp.exp2(-jnp.abs(z)))
    ls_pos = jnp.minimum(z, 0.0) - sp
    return ls_pos, ls_pos - z


def _split_bf16(x):
    hi = x.astype(BF16)
    lo = (x - hi.astype(F32)).astype(BF16)
    return hi, lo


def _sb_wave_kernel(qtab, ktab, bias_ref, q_ref, k_ref, v_ref, suf_ref, o_ref, lsp, lsn, rowsum, run, acc,
                    *, tile, n_qt, n_trips):
    head = pl.program_id(1)
    bias = bias_ref[head]
    suf = suf_ref[...]
    earlier = (lax.broadcasted_iota(jnp.int32, (tile, tile), 1) < lax.broadcasted_iota(jnp.int32, (tile, tile), 0))
    per_trip = ATT_TILES_PER_TRIP

    def terms(qi, kj, buf, slot, diagonal):
        q0 = pl.multiple_of(qi * tile, tile)
        k0 = pl.multiple_of(kj * tile, tile)
        z = _dot_nt(q_ref[pl.ds(q0, tile), :], k_ref[pl.ds(k0, tile), :]) + bias
        ls_pos, ls_neg = _sb_terms2(z)
        if diagonal:
            ls_neg = jnp.where(earlier, ls_neg, 0.0)
            ls_pos = jnp.where(earlier, ls_pos, -jnp.inf)
        lsp[buf, slot] = ls_pos
        lsn[buf, slot] = ls_neg.astype(BF16)
        rowsum[buf, slot] = jnp.sum(ls_neg, axis=-1, keepdims=True)

    def apply(si, kj, buf, slot, first):
        k0 = pl.multiple_of(kj * tile, tile)
        within = _dot(lsn[buf, slot], suf)
        if first:
            logw = lsp[buf, slot] + within
        else:
            logw = lsp[buf, slot] + (run[si] + within)
        wv = _dot(jnp.exp2(logw).astype(BF16), v_ref[pl.ds(k0, tile), :])
        if first:
            acc[si] = wv
            run[si] = rowsum[buf, slot]
        else:
            acc[si] += wv
            run[si] += rowsum[buf, slot]

    def diag_body(g, carry):
        for buf in range(2):
            for slot in range(per_trip):
                qi = (2 * g + buf) * per_trip + slot
                terms(qi, qi, buf, slot, True)
        for buf in range(2):
            for slot in range(per_trip):
                qi = (2 * g + buf) * per_trip + slot
                apply(qi, qi, buf, slot, True)
        return carry

    lax.fori_loop(0, n_qt // (2 * per_trip), diag_body, 0)
    run[n_qt] = jnp.zeros((tile, 1), F32)
    acc[n_qt] = jnp.zeros((tile, HEAD_DIM), F32)

    def trip_terms(t, buf):
        for slot in range(per_trip):
            f = t * per_trip + slot
            terms(jnp.minimum(qtab[f], n_qt - 1), ktab[f], buf, slot, False)

    def trip_apply(t, buf):
        for slot in range(per_trip):
            f = t * per_trip + slot
            apply(qtab[f], ktab[f], buf, slot, False)

    trip_terms(0, 0)

    def body(i, carry):
        t = 2 * i
        trip_terms(t + 1, 1)
        trip_apply(t, 0)
        trip_terms(t + 2, 0)
        trip_apply(t + 1, 1)
        return carry

    lax.fori_loop(0, n_trips // 2 - 1, body, 0)
    trip_terms(n_trips - 1, 1)
    trip_apply(n_trips - 2, 0)
    trip_apply(n_trips - 1, 1)
    o_ref[...] = acc[0:n_qt].reshape(n_qt * tile, HEAD_DIM).astype(o_ref.dtype)


def _sb_prompt_attention(q, k, v, logit_bias, *, n_seq, seq_len):
    t, d = q.shape
    n_heads = d // HEAD_DIM
    tile = ATT_TILE
    per_trip = ATT_TILES_PER_TRIP
    n_qt = seq_len // tile
    assert seq_len % tile == 0 and n_qt % (2 * per_trip) == 0
    tiles = [(qi, qi - dist) for dist in range(1, n_qt) for qi in range(dist, n_qt)]
    n_trips = max(2, 2 * -(-len(tiles) // (2 * per_trip)))
    tiles += [(n_qt, 0)] * (n_trips * per_trip - len(tiles))
    qtab = jnp.asarray([qi for qi, _ in tiles], jnp.int32)
    ktab = jnp.asarray([kj for _, kj in tiles], jnp.int32)
    suf = (jnp.arange(tile)[:, None] > jnp.arange(tile)[None, :]).astype(BF16)
    seq_spec = pl.BlockSpec((seq_len, HEAD_DIM), lambda b, h, qt, kt: (b, h))
    grid_spec = pltpu.PrefetchScalarGridSpec(
        num_scalar_prefetch=2,
        grid=(n_seq, n_heads),
        in_specs=[
            pl.BlockSpec(memory_space=pltpu.SMEM),
            seq_spec, seq_spec, seq_spec,
            pl.BlockSpec((tile, tile), lambda b, h, qt, kt: (0, 0)),
        ],
        out_specs=seq_spec,
        scratch_shapes=[
            pltpu.VMEM((2, per_trip, tile, tile), F32),
            pltpu.VMEM((2, per_trip, tile, tile), BF16),
            pltpu.VMEM((2, per_trip, tile, 1), F32),
            pltpu.VMEM((n_qt + 1, tile, 1), F32),
            pltpu.VMEM((n_qt + 1, tile, HEAD_DIM), F32),
        ],
    )
    return pl.pallas_call(
        functools.partial(_sb_wave_kernel, tile=tile, n_qt=n_qt, n_trips=n_trips),
        grid_spec=grid_spec,
        out_shape=jax.ShapeDtypeStruct((t, d), BF16),
        compiler_params=_params("parallel", "arbitrary"),
        name="sb_prompt_attention",
    )(qtab, ktab, logit_bias.astype(F32) * LOG2E, q, k, v, suf)


def _sb_sample_kernel(pt_ref, q_ref, kn_ref, vn_ref, *refs, n_heads, n_q, n_new, page, n_steps, per_step):
    del pt_ref
    kc_refs, vc_refs = refs[:per_step], refs[per_step:2 * per_step]
    bias_ref, scan_ref, o_ref, run, acc = refs[2 * per_step:]
    step = pl.program_id(1)
    heads = range(n_heads)
    scan = scan_ref[...]
    qrow = lax.broadcasted_iota(jnp.int32, (SUBLANES, HEAD_DIM), 0)
    kcol = lax.broadcasted_iota(jnp.int32, (SUBLANES, HEAD_DIM), 1)
    live_q = qrow < n_q
    q = [q_ref[0, h].astype(BF16) for h in heads]

    def log_weights(keys, live):
        units = [(g, h) for g in range(len(keys)) for h in heads]
        z = [_dot_nt(q[h], keys[g][h]) + bias_ref[h] for g, h in units]
        lt = [_sb_terms2(zz) for zz in z]
        ls_neg = [jnp.where(live, t[1], 0.0) for t in lt]
        rel, tot = {}, {}
        for g in range(len(keys)):
            hi, lo = _split_bf16(jnp.concatenate(ls_neg[g * n_heads:(g + 1) * n_heads], axis=0))
            res = _dot(hi, scan) + _dot(lo, scan)
            for h in heads:
                blk = res[h * SUBLANES:(h + 1) * SUBLANES, :]
                rel[g, h] = lt[g * n_heads + h][0] + blk[:, 0:HEAD_DIM]
                tot[g, h] = blk[:, HEAD_DIM:]
        return rel, tot

    def weighted_values(rel, run_val, live, values):
        w = jnp.exp2(jnp.where(live, rel + run_val, -jnp.inf)).astype(BF16)
        return _dot(w, values)

    def strided_head(ref, h):
        return ref[0, pl.ds(h, page, stride=n_heads), :].astype(BF16)

    @pl.when(step == 0)
    def _():
        live = live_q & (kcol < qrow) & (kcol < n_new)
        zpad = jnp.zeros((HEAD_DIM - SUBLANES, HEAD_DIM), F32)
        pad = lambda ref, h: jnp.concatenate([ref[0, h], zpad], axis=0).astype(BF16)
        rel, tot = log_weights([[pad(kn_ref, h) for h in heads]], live)
        for h in heads:
            acc[h] = weighted_values(rel[0, h], jnp.zeros((SUBLANES, HEAD_DIM), F32), live, pad(vn_ref, h))
            run[h] = tot[0, h]

    rel, tot = log_weights([[strided_head(kc_refs[g], h) for h in heads] for g in range(per_step)], live_q)
    for h in heads:
        run_val = run[h]
        acc_val = acc[h]
        for g in range(per_step):
            acc_val = acc_val + weighted_values(rel[g, h], run_val, live_q, strided_head(vc_refs[g], h))
            run_val = run_val + tot[g, h]
        run[h] = run_val
        acc[h] = acc_val

    @pl.when(step == n_steps - 1)
    def _():
        o_ref[0] = acc[...]


def _sb_sample_attention(q, k_new, v_new, cache_k, cache_v, page_table, logit_bias, *, n_seq, n_q):
    t, d = q.shape
    n_heads = d // HEAD_DIM
    n_phys, page = cache_k.shape[0], cache_k.shape[1]
    n_pages = page_table.shape[1]
    assert page == HEAD_DIM and n_q <= SUBLANES
    by_head = lambda a: jnp.pad(a.astype(F32).reshape(n_seq, n_q, n_heads, HEAD_DIM).transpose(0, 2, 1, 3),
                                ((0, 0), (0, 0), (0, SUBLANES - n_q), (0, 0)))
    kc = cache_k.reshape(n_phys, page * n_heads, HEAD_DIM)
    vc = cache_v.reshape(n_phys, page * n_heads, HEAD_DIM)
    key = jnp.arange(HEAD_DIM)
    scan = jnp.concatenate([key[:, None] > key[None, :], jnp.ones((HEAD_DIM, HEAD_DIM), bool)], axis=1).astype(BF16)
    per_step = math.gcd(n_pages, SAMPLE_PAGES_PER_STEP)
    n_steps = n_pages // per_step
    seq4 = lambda b, j, pt: (b, 0, 0, 0)
    head_spec = pl.BlockSpec((1, n_heads, SUBLANES, HEAD_DIM), seq4)
    page_specs = [pl.BlockSpec((1, page * n_heads, HEAD_DIM),
                               lambda b, j, pt, i=i: (pt[b, n_pages - 1 - (j * per_step + i)], 0, 0))
                  for i in range(per_step)]
    grid_spec = pltpu.PrefetchScalarGridSpec(
        num_scalar_prefetch=1,
        grid=(n_seq, n_steps),
        in_specs=[head_spec] * 3 + page_specs + page_specs + [
            pl.BlockSpec(memory_space=pltpu.SMEM),
            pl.BlockSpec((HEAD_DIM, 2 * HEAD_DIM), lambda b, j, pt: (0, 0)),
        ],
        out_specs=head_spec,
        scratch_shapes=[pltpu.VMEM((n_heads, SUBLANES, HEAD_DIM), F32)] * 2,
    )
    out = pl.pallas_call(
        functools.partial(_sb_sample_kernel, n_heads=n_heads, n_q=n_q, n_new=n_q, page=page, n_steps=n_steps,
                          per_step=per_step),
        grid_spec=grid_spec,
        out_shape=jax.ShapeDtypeStruct((n_seq, n_heads, SUBLANES, HEAD_DIM), F32),
        compiler_params=_params("parallel", "arbitrary"),
        name="sb_sample_attention",
    )(page_table, by_head(q), by_head(k_new), by_head(v_new), *([kc] * per_step), *([vc] * per_step),
      logit_bias.astype(F32) * LOG2E, scan)
    return out[:, :, :n_q].transpose(0, 2, 1, 3).reshape(t, d).astype(BF16)


def _trunk(x, tail, s0, past, w, *, n_seq, seq_len):
    d = x.shape[1]
    x = _ffn(x, w["ffn1_norm"], w["ffn1_w_gu"], w["ffn1_w_down"], 0)
    proj = _normproj(x, w["mix_norm"][0], w["gdn_w_in"])
    seq_rows = -(-seq_len // SUBLANES) * SUBLANES
    chunk = min(GDN_CHUNK, seq_rows)
    pad = lambda a: jnp.pad(a.reshape(n_seq, seq_len, -1), ((0, 0), (0, seq_rows - seq_len), (0, 0))
                            ).reshape(n_seq * seq_rows, -1)
    xg, pg = (x, proj) if seq_rows == seq_len else (pad(x), pad(proj))
    xg, state = _gdn_layer(xg, pg, tail, s0, w["gdn_conv_w"], w["gdn_a_log"], w["gdn_dt_bias"], w["gdn_o_norm"],
                           w["gdn_w_out"], n_seq=n_seq, seq_rows=seq_rows, chunk=chunk,
                           valid_rows=min(seq_len, chunk))
    x = xg if seq_rows == seq_len else xg.reshape(n_seq, seq_rows, d)[:, :seq_len].reshape(n_seq * seq_len, d)
    x = _ffn(x, w["ffn2_norm"], w["ffn2_w_gu"], w["ffn2_w_down"], 0)
    k, v, kb, vb = _shared_kv(x, w["kv_norm"], w["w_kv"], w["k_norm"])
    x, q = _ffn(x, w["ffn1_norm"], w["ffn1_w_gu"], w["ffn1_w_down"], 1,
                q_proj=(w["mix_norm"][1], w["sb_w_q"], w["sb_q_norm"]))
    if past is None:
        att = _sb_prompt_attention(q, kb, vb, w["sb_logit_bias"], n_seq=n_seq, seq_len=seq_len)
    else:
        att = _sb_sample_attention(q, kb, vb, past[0], past[1], past[2], w["sb_logit_bias"], n_seq=n_seq, n_q=seq_len)
    x = _ffn(x, w["ffn2_norm"], w["ffn2_w_gu"], w["ffn2_w_down"], 1, mixer_out=(att, w["sb_w_out"]))
    return x, proj.reshape(n_seq, seq_len, -1), state, k, v


def kernel(x_prompt, x_sample, state_gdn, state_conv, cache_k, cache_v, page_table, ffn1_norm, ffn1_w_gu,
           ffn1_w_down, ffn2_norm, ffn2_w_gu, ffn2_w_down, mix_norm, gdn_w_in, gdn_conv_w, gdn_a_log, gdn_dt_bias,
           gdn_o_norm, gdn_w_out, kv_norm, w_kv, k_norm, sb_w_q, sb_q_norm, sb_logit_bias, sb_w_out):
    bp, seq, d = x_prompt.shape
    bs, dec = x_sample.shape[0], x_sample.shape[1]
    n_heads = d // HEAD_DIM
    hist = state_conv.shape[2]
    assert gdn_w_in.shape[0] == 1 and sb_w_q.shape[0] == 1, "one layer of each mixer kind"
    assert dec <= SUBLANES and hist < SUBLANES and dec >= hist

    in_cols = gdn_w_in.shape[2]
    np_cols = -(-in_cols // HEAD_DIM) * HEAD_DIM
    w = {
        "ffn1_norm": ffn1_norm, "ffn2_norm": ffn2_norm, "mix_norm": mix_norm,
        "ffn1_w_gu": ffn1_w_gu.astype(BF16), "ffn1_w_down": ffn1_w_down.astype(BF16),
        "ffn2_w_gu": ffn2_w_gu.astype(BF16), "ffn2_w_down": ffn2_w_down.astype(BF16),
        "gdn_w_in": jnp.pad(gdn_w_in[0], ((0, 0), (0, np_cols - in_cols))).astype(BF16),
        "gdn_conv_w": gdn_conv_w[0], "gdn_a_log": gdn_a_log[0], "gdn_dt_bias": gdn_dt_bias[0],
        "gdn_o_norm": gdn_o_norm[0], "gdn_w_out": gdn_w_out[0].astype(BF16),
        "kv_norm": kv_norm, "w_kv": w_kv.astype(BF16), "k_norm": k_norm,
        "sb_w_q": sb_w_q[0].astype(BF16), "sb_q_norm": sb_q_norm[0], "sb_logit_bias": sb_logit_bias[0],
        "sb_w_out": sb_w_out[0].astype(BF16),
    }

    zero_tail = jnp.zeros((bp, SUBLANES, 3 * d), F32)
    zero_state = jnp.zeros((bp, n_heads, HEAD_DIM, HEAD_DIM), F32)
    y_p, qkv_p, st_p, k_p, v_p = _trunk(x_prompt.reshape(bp * seq, d), zero_tail, zero_state, None, w,
                                        n_seq=bp, seq_len=seq)

    tail_s = jnp.pad(state_conv[0], ((0, 0), (SUBLANES - hist, 0), (0, 0)))
    y_s, qkv_s, st_s, k_s, v_s = _trunk(x_sample.reshape(bs * dec, d), tail_s, state_gdn[0],
                                        (cache_k, cache_v, page_table), w, n_seq=bs, seq_len=dec)

    conv_p = qkv_p[:, seq - hist:seq, :3 * d][None]
    conv_s = qkv_s[:, dec - hist:dec, :3 * d][None]
    return (y_p.reshape(bp, seq, d), y_s.reshape(bs, dec, d), st_p[None].astype(x_prompt.dtype), conv_p,
            k_p.reshape(bp, seq, n_heads, HEAD_DIM), v_p.reshape(bp, seq, n_heads, HEAD_DIM),
            st_s[None].astype(x_sample.dtype), conv_s,
            k_s.reshape(bs, dec, n_heads, HEAD_DIM), v_s.reshape(bs, dec, n_heads, HEAD_DIM))
```

```python
import functools
import math

import jax
import jax.numpy as jnp
from jax import lax
from jax.experimental import pallas as pl
from jax.experimental.pallas import tpu as pltpu

F32 = jnp.float32
BF16 = jnp.bfloat16

EPS = 1e-6
HEAD_DIM = 128
GDN_CHUNK = 64
GDN_CHUNKS_PER_TRIP = 4
SUBLANES = 8
TOKEN_TILE = 512
FFN_ROW_TILE = 1024
FFN_CHUNK = 256
ATT_TILE = 256
ATT_TILES_PER_TRIP = 4
SAMPLE_PAGES_PER_STEP = 8
SAMPLE_RING = 3
SAMPLE_SEQS_PER_STEP = 4
VMEM_LIMIT = 56 * 1024 * 1024
PREC_EXACT = lax.Precision.HIGHEST
LOG2E = math.log2(math.e)
Q_SCALE = HEAD_DIM ** -0.5 * LOG2E


def _params(*sem):
    return pltpu.CompilerParams(dimension_semantics=sem, vmem_limit_bytes=VMEM_LIMIT)


def _dot(a, b):
    return jnp.dot(a, b, preferred_element_type=F32)


def _dot_nt(a, b):
    return lax.dot_general(a, b, (((1,), (1,)), ((), ())), preferred_element_type=F32)


def _dot_exact(a, b):
    return jnp.dot(a, b, preferred_element_type=F32, precision=PREC_EXACT)


def _rms(x, w):
    return x * lax.rsqrt(jnp.mean(x * x, axis=-1, keepdims=True) + EPS) * w


def _silu(x):
    return x * jax.nn.sigmoid(x)


def _softplus_neg_abs(x):
    return jnp.log1p(jnp.exp(-jnp.abs(x)))


def _row_tile(n_rows):
    return TOKEN_TILE if n_rows % TOKEN_TILE == 0 else n_rows


def _head_rms(y, w, h):
    seg = y[:, h * HEAD_DIM:(h + 1) * HEAD_DIM]
    return seg * lax.rsqrt(jnp.mean(seg * seg, axis=-1, keepdims=True) + EPS) * w


def _ffn_kernel(*refs, d_ff, n_split, mixer_out, q_heads):
    refs = list(refs)
    x_ref = refs.pop(0)
    a_ref, wo_ref = (refs.pop(0), refs.pop(0)) if mixer_out else (None, None)
    nw_ref, wgu_ref, wd_ref = refs.pop(0), refs.pop(0), refs.pop(0)
    mn_ref, wq_ref, qn_ref = (refs.pop(0), refs.pop(0), refs.pop(0)) if q_heads else (None, None, None)
    o_ref = refs.pop(0)
    x = x_ref[...]
    if mixer_out:
        x = x + _dot(a_ref[...], wo_ref[...])
    h = _rms(x, nw_ref[...]).astype(BF16)
    fc = d_ff // n_split
    acc = jnp.zeros_like(x)
    for c in range(n_split):
        g = _dot(h, wgu_ref[:, c * fc:(c + 1) * fc])
        u = _dot(h, wgu_ref[:, d_ff + c * fc:d_ff + (c + 1) * fc])
        a = (_silu(g) * u).astype(BF16)
        acc = acc + _dot(a, wd_ref[c * fc:(c + 1) * fc, :])
    y = x + 0.5 * acc
    o_ref[...] = y
    if q_heads:
        q_ref = refs.pop(0)
        q = _dot(_rms(y, mn_ref[...]).astype(BF16), wq_ref[...])
        for hd in range(q_heads):
            q_ref[:, hd * HEAD_DIM:(hd + 1) * HEAD_DIM] = (_head_rms(q, qn_ref[...], hd) * Q_SCALE).astype(BF16)


def _ffn(x, norm_w, w_gu, w_down, layer, mixer_out=None, q_proj=None):
    t, d = x.shape
    d_ff = w_down.shape[1]
    tm = FFN_ROW_TILE if t % FFN_ROW_TILE == 0 else _row_tile(t)
    n_split = d_ff // FFN_CHUNK if d_ff % FFN_CHUNK == 0 else 1
    of_layer = lambda i: (layer, 0, 0)
    const = lambda i: (0, 0)
    row = pl.BlockSpec((tm, d), lambda i: (i, 0))
    square =pl.BlockSpec((d, d), const, pipeline_mode=pl.Buffered(1))
    args, specs = [x], [row]
    if mixer_out is not None:
        args += list(mixer_out)
        specs += [row, square]
    args += [norm_w.reshape(-1, 1, d), w_gu, w_down]
    specs += [
        pl.BlockSpec((None, 1, d), of_layer),
        pl.BlockSpec((None, d, 2 * d_ff), of_layer, pipeline_mode=pl.Buffered(1)),
        pl.BlockSpec((None, d_ff, d), of_layer, pipeline_mode=pl.Buffered(1)),
    ]
    out_specs, out_shape = row, jax.ShapeDtypeStruct((t, d), F32)
    if q_proj is not None:
        args += [q_proj[0].reshape(1, d), q_proj[1], q_proj[2].reshape(1, HEAD_DIM)]
        specs += [pl.BlockSpec((1, d), const), square, pl.BlockSpec((1, HEAD_DIM), const)]
        out_specs, out_shape = [row, row], [out_shape, jax.ShapeDtypeStruct((t, d), BF16)]
    return pl.pallas_call(
        functools.partial(_ffn_kernel, d_ff=d_ff, n_split=n_split, mixer_out=mixer_out is not None,
                          q_heads=d // HEAD_DIM if q_proj is not None else 0),
        grid=(t // tm,),
        in_specs=specs,
        out_specs=out_specs,
        out_shape=out_shape,
        compiler_params=_params("parallel"),
        name="ffn",
    )(*args)


def _normproj_kernel(x_ref, nw_ref, w_ref, o_ref):
    h = _rms(x_ref[...], nw_ref[...]).astype(BF16)
    o_ref[...] = _dot(h, w_ref[...])


def _normproj(x, norm_w, w):
    t, d = x.shape
    n = w.shape[1]
    tm = _row_tile(t)
    const = lambda i: (0, 0)
    return pl.pallas_call(
        _normproj_kernel,
        grid=(t // tm,),
        in_specs=[
            pl.BlockSpec((tm, d), lambda i: (i, 0)),
            pl.BlockSpec((1, d), const),
            pl.BlockSpec((d, n), const, pipeline_mode=pl.Buffered(1)),
        ],
        out_specs=pl.BlockSpec((tm, n), lambda i: (i, 0)),
        out_shape=jax.ShapeDtypeStruct((t, n), F32),
        compiler_params=_params("parallel"),
        name="gdn_in_proj",
    )(x, norm_w.reshape(1, d), w)


def _kv_kernel(x_ref, nw_ref, w_ref, kn_ref, k_ref, v_ref, kb_ref, vb_ref, *, n_heads):
    hid = _rms(x_ref[...], nw_ref[...]).astype(BF16)
    kv = _dot(hid, w_ref[...])
    d = n_heads * HEAD_DIM
    for h in range(n_heads):
        sl = slice(h * HEAD_DIM, (h + 1) * HEAD_DIM)
        k = _head_rms(kv, kn_ref[...], h)
        k_ref[:, sl] = k
        kb_ref[:, sl] = k.astype(BF16)
    v = kv[:, d:]
    v_ref[...] = v
    vb_ref[...] = v.astype(BF16)


def _shared_kv(x, kv_norm, w_kv, k_norm):
    t, d = x.shape
    n_heads = d // HEAD_DIM
    tm = _row_tile(t)
    const = lambda i: (0, 0)
    row = pl.BlockSpec((tm, d), lambda i: (i, 0))
    return pl.pallas_call(
        functools.partial(_kv_kernel, n_heads=n_heads),
        grid=(t // tm,),
        in_specs=[
            row,
            pl.BlockSpec((1, d), const),
            pl.BlockSpec((d, 2 * d), const, pipeline_mode=pl.Buffered(1)),
            pl.BlockSpec((1, HEAD_DIM), const),
        ],
        out_specs=[row, row, row, row],
        out_shape=[jax.ShapeDtypeStruct((t, d), F32), jax.ShapeDtypeStruct((t, d), F32),
                   jax.ShapeDtypeStruct((t, d), BF16), jax.ShapeDtypeStruct((t, d), BF16)],
        compiler_params=_params("parallel"),
        name="shared_kv",
    )(x, kv_norm.reshape(1, d), w_kv, k_norm.reshape(1, HEAD_DIM))


def _unit_lower_inverses(ms, c, merge_masks):
    n_blk = c // SUBLANES
    col_id = lax.broadcasted_iota(jnp.int32, (SUBLANES, c), 1)
    row_id = lax.broadcasted_iota(jnp.int32, (SUBLANES, c), 0)
    m_blk = [[m[b * SUBLANES:(b + 1) * SUBLANES, :] for b in range(n_blk)] for m in ms]
    t_blk = [[(col_id == row_id + b * SUBLANES).astype(F32) for b in range(n_blk)] for _ in ms]
    for j in range(SUBLANES - 1):
        for i in range(len(ms)):
            for b in range(n_blk):
                col = m_blk[i][b][:, b * SUBLANES + j:b * SUBLANES + j + 1]
                t_blk[i][b] = t_blk[i][b] - col * t_blk[i][b][j:j + 1, :]
    ts = [jnp.concatenate(t, axis=0) if n_blk > 1 else t[0] for t in t_blk]
    for off_diag in merge_masks:
        cts = [_dot(jnp.where(off_diag, m, 0.0).astype(BF16), t.astype(BF16)) for m, t in zip(ms, ts)]
        ts = [t - _dot(t.astype(BF16), ct.astype(BF16)) for t, ct in zip(ts, cts)]
    return ts


def _merge_masks(c):
    ri = lax.broadcasted_iota(jnp.int32, (c, c), 0)
    ci = lax.broadcasted_iota(jnp.int32, (c, c), 1)
    masks = []
    shift = int(math.log2(SUBLANES))
    while (1 << shift) < c:
        same_pair = lax.shift_right_logical(ri, shift + 1) == lax.shift_right_logical(ci, shift + 1)
        same_blk = lax.shift_right_logical(ri, shift) == lax.shift_right_logical(ci, shift)
        masks.append(same_pair & jnp.logical_not(same_blk))
        shift += 1
    return masks


def _gdn_kernel(proj_ref, tail_ref, s0_ref, x_ref, cw_ref, alog_ref, dtb_ref, onorm_ref, wout_ref,
                xo_ref, sf_ref, state, xc, tail_s, og, *, n_heads, chunk, rows, valid_rows, n_blocks, n_seqs):
    qk_dim = n_heads * HEAD_DIM
    blk = pl.program_id(1)

    @pl.when(blk == 0)
    def _():
        state[...] = s0_ref[...]
        tail_s[...] = tail_ref[...]

    for sq in range(n_seqs):
        xc[sq, 0:SUBLANES, :] = tail_s[sq]
        xc[sq, SUBLANES:SUBLANES + rows, :] = proj_ref[sq * rows:(sq + 1) * rows, 0:3 * qk_dim]
        tail_s[sq] = xc[sq, rows:rows + SUBLANES, :]

    c = chunk
    ri = lax.broadcasted_iota(jnp.int32, (c, c), 0)
    ci = lax.broadcasted_iota(jnp.int32, (c, c), 1)
    lower_incl = (ri >= ci)
    lower_strict = (ri > ci)
    cum_mat = lower_incl.astype(F32)
    merge_masks = _merge_masks(c)
    neg_a =-jnp.exp(alog_ref[...])
    dtb = dtb_ref[...]
    onorm = onorm_ref[...]
    conv_taps = cw_ref.shape[0]

    def conv_seg(sq, r0, col0):
        win = xc[sq, pl.ds(r0, c + SUBLANES), col0:col0 + HEAD_DIM]
        acc = None
        for w in range(conv_taps):
            shift = conv_taps - 1 - w
            xw = win if shift == 0 else pltpu.roll(win, shift, axis=0)
            term = xw[SUBLANES:, :] * cw_ref[w:w + 1, col0:col0 + HEAD_DIM]
            acc = term if acc is None else acc + term
        return _silu(acc)

    def l2n(v):
        return v * lax.rsqrt(jnp.sum(v * v, axis=-1, keepdims=True) + EPS)

    def chunk_body(seqs, r0s):
        gates = {}
        for sq, ci in [(sq, ci) for ci in range(len(r0s)) for sq in seqs]:
            gate = proj_ref[pl.ds(sq * rows + r0s[ci], c), 4 * qk_dim:4 * qk_dim + HEAD_DIM]
            beta_all = jax.nn.sigmoid(gate)
            gpre = gate + dtb
            g_all = neg_a * (jnp.maximum(gpre, 0.0) + _softplus_neg_abs(gpre))
            if valid_rows < c:
                live = lax.broadcasted_iota(jnp.int32, (c, HEAD_DIM), 0) < valid_rows
                beta_all = jnp.where(live, beta_all, 0.0)
                g_all = jnp.where(live, g_all, 0.0)
            gc_all = _dot_exact(cum_mat, g_all)
            gates[sq, ci] = (beta_all, gc_all, gc_all.T)
        units = [(sq, h, ci) for ci in range(len(r0s)) for sq in seqs for h in range(n_heads)]
        q = [l2n(conv_seg(sq, r0s[ci], h * HEAD_DIM)) * (HEAD_DIM ** -0.5) for sq, h, ci in units]
        k = [l2n(conv_seg(sq, r0s[ci], qk_dim + h * HEAD_DIM)) for sq, h, ci in units]
        v = [conv_seg(sq, r0s[ci], 2 * qk_dim + h * HEAD_DIM) for sq, h, ci in units]
        lanes = lambda col: jnp.broadcast_to(col, (c, HEAD_DIM))
        beta = [lanes(gates[sq, ci][0][:, h:h + 1]) for sq, h, ci in units]
        gcol = [lanes(gates[sq, ci][1][:, n_heads + h:n_heads + h + 1]) for sq, h, ci in units]
        grow = [gates[sq, ci][2][n_heads + h:n_heads + h + 1, :] for sq, h, ci in units]
        n = range(len(units))
        decay = [jnp.exp(jnp.where(lower_incl, gcol[i][:, 0:c] - grow[i], -jnp.inf)) for i in n]
        kq = [jnp.concatenate([k[i], q[i]], axis=0).astype(BF16) for i in n]
        kk_qk = [_dot_nt(kq[i], kq[i][0:c, :]) for i in n]
        m = [jnp.where(lower_strict, beta[i][:, 0:c] * kk_qk[i][0:c, :] * decay[i], 0.0) for i in n]
        t_inv = _unit_lower_inverses(m, c, merge_masks)
        gam = [jnp.exp(gcol[i]) for i in n]
        rhs = [jnp.concatenate([(beta[i] * gam[i]) * k[i], beta[i] * v[i]], axis=1).astype(BF16) for i in n]
        wu = [_dot(t_inv[i].astype(BF16), rhs[i]) for i in n]
        g_last = [gcol[i][c - 1:c, :] for i in n]
        lhs2 = [jnp.concatenate([kk_qk[i][c:2 * c, :] * decay[i], (k[i] * jnp.exp(g_last[i] - gcol[i])).T],
                                axis=0).astype(BF16) for i in n]
        lhs1 = [jnp.concatenate([wu[i][:, 0:HEAD_DIM], q[i] * gam[i]], axis=0).astype(BF16) for i in n]
        for cj in range(len(r0s)):
            mine = [i for i in n if units[i][2] == cj]
            s = {i: state[units[i][0], units[i][1]] for i in mine}
            ws = {i: _dot(lhs1[i], s[i].astype(BF16)) for i in mine}
            ub = {i: (wu[i][:, HEAD_DIM:] - ws[i][0:c, :]).astype(BF16) for i in mine}
            upd = {i: _dot(lhs2[i], ub[i]) for i in mine}
            for i in mine:
                sq, h, _ = units[i]
                state[sq, h] = jnp.exp(g_last[i]) * s[i] + upd[i][c:, :]
                o = ws[i][c:2 * c, :] + upd[i][0:c, :]
                base = sq * rows + r0s[cj]
                z = proj_ref[pl.ds(base, c), 3 * qk_dim + h * HEAD_DIM:3 * qk_dim + (h + 1) * HEAD_DIM]
                og[pl.ds(base, c), h * HEAD_DIM:(h + 1) * HEAD_DIM] = _rms(o, onorm) * _silu(z)

    n_chunks = rows // c
    per_trip = GDN_CHUNKS_PER_TRIP if n_chunks % GDN_CHUNKS_PER_TRIP == 0 else 1
    if n_chunks == 1:
        chunk_body(list(range(n_seqs)), [0])
    else:
        def loop_body(trip, carry):
            r0 = pl.multiple_of(trip * (per_trip * c), per_trip * c)
            chunk_body(list(range(n_seqs)), [r0 + j * c for j in range(per_trip)])
            return carry
        lax.fori_loop(0, n_chunks // per_trip, loop_body, 0)
    xo_ref[...] = x_ref[...] + _dot(og[...].astype(BF16), wout_ref[...])

    @pl.when(blk == n_blocks - 1)
    def _():
        sf_ref[...] = state[...]


def _gdn_layer(x, proj, tail, s0, conv_w, a_log, dt_bias, o_norm, w_out, *, n_seq, seq_rows, chunk, valid_rows):
    t, d = x.shape
    n_heads = d // HEAD_DIM
    np_cols = proj.shape[1]
    rows = min(seq_rows, TOKEN_TILE)
    n_blocks = seq_rows // rows
    n_seqs = 1
    if n_blocks == 1 and n_seq % SAMPLE_SEQS_PER_STEP == 0:
        n_seqs = SAMPLE_SEQS_PER_STEP
    lane_pad = HEAD_DIM - 2 * n_heads
    alog_l = jnp.pad(a_log, (n_heads, lane_pad)).reshape(1, HEAD_DIM)
    dtb_l = jnp.pad(dt_bias, (n_heads, lane_pad)).reshape(1, HEAD_DIM)
    const2 = lambda b, r: (0, 0)
    row_map = lambda b, r: (b * n_blocks + r, 0)
    seq_map4 = lambda b, r: (b, 0, 0, 0)
    return pl.pallas_call(
        functools.partial(_gdn_kernel, n_heads=n_heads, chunk=chunk, rows=rows, valid_rows=valid_rows,
                          n_blocks=n_blocks, n_seqs=n_seqs),
        grid=(n_seq // n_seqs, n_blocks),
        in_specs=[
            pl.BlockSpec((n_seqs * rows, np_cols), row_map),
            pl.BlockSpec((n_seqs, SUBLANES, 3 * d), lambda b, r: (b, 0, 0)),
            pl.BlockSpec((n_seqs, n_heads, HEAD_DIM, HEAD_DIM), seq_map4),
            pl.BlockSpec((n_seqs * rows, d), row_map),
            pl.BlockSpec(conv_w.shape, const2),
            pl.BlockSpec((1, HEAD_DIM), const2),
            pl.BlockSpec((1, HEAD_DIM), const2),
            pl.BlockSpec((1, HEAD_DIM), const2),
            pl.BlockSpec((d, d), const2, pipeline_mode=pl.Buffered(1)),
        ],
        out_specs=[
            pl.BlockSpec((n_seqs * rows, d), row_map),
            pl.BlockSpec((n_seqs, n_heads, HEAD_DIM, HEAD_DIM), seq_map4),
        ],
        out_shape=[jax.ShapeDtypeStruct((t, d), F32),
                   jax.ShapeDtypeStruct((n_seq, n_heads, HEAD_DIM, HEAD_DIM), F32)],
        scratch_shapes=[
            pltpu.VMEM((n_seqs, n_heads, HEAD_DIM, HEAD_DIM), F32),
            pltpu.VMEM((n_seqs, rows + SUBLANES, 3 * d), F32),
            pltpu.VMEM((n_seqs, SUBLANES, 3 * d), F32),
            pltpu.VMEM((n_seqs * rows, d), F32),
        ],
        compiler_params=_params("arbitrary", "arbitrary"),
        name="gdn_mixer",
    )(proj, tail, s0, x, conv_w, alog_l, dtb_l, o_norm.reshape(1, HEAD_DIM), w_out)


def _sb_terms2(z):
    sp = jnp.log2(1.0 + jnp.exp2(-jnp.abs(z)))
    ls_pos = jnp.minimum(z, 0.0) - sp
    return ls_pos, ls_pos - z


def _split_bf16(x):
    hi = x.astype(BF16)
    lo = (x - hi.astype(F32)).astype(BF16)
    return hi, lo


def _sb_wave_kernel(qtab, ktab, bias_ref, q_ref, k_ref, v_ref, suf_ref, o_ref, lsp, lsn, rowsum, run, acc,
                    *, tile, n_qt, n_trips):
    head = pl.program_id(1)
    bias = bias_ref[head]
    suf = suf_ref[...]
    earlier = (lax.broadcasted_iota(jnp.int32, (tile, tile), 1) < lax.broadcasted_iota(jnp.int32, (tile, tile), 0))
    per_trip = ATT_TILES_PER_TRIP

    def terms(qi, kj, buf, slot, diagonal):
        q0 = pl.multiple_of(qi * tile, tile)
        k0 = pl.multiple_of(kj * tile, tile)
        z = _dot_nt(q_ref[pl.ds(q0, tile), :], k_ref[pl.ds(k0, tile), :]) + bias
        ls_pos, ls_neg = _sb_terms2(z)
        if diagonal:
            ls_neg = jnp.where(earlier, ls_neg, 0.0)
            ls_pos = jnp.where(earlier, ls_pos, -jnp.inf)
        lsp[buf, slot] = ls_pos
        lsn[buf, slot] = ls_neg.astype(BF16)
        rowsum[buf, slot] = jnp.sum(ls_neg, axis=-1, keepdims=True)

    def apply(si, kj, buf, slot, first):
        k0 = pl.multiple_of(kj * tile, tile)
        within = _dot(lsn[buf, slot], suf)
        if first:
            logw = lsp[buf, slot] + within
        else:
            logw = lsp[buf, slot] + (run[si] + within)
        wv = _dot(jnp.exp2(logw).astype(BF16), v_ref[pl.ds(k0, tile), :])
        if first:
            acc[si] = wv
            run[si] = rowsum[buf, slot]
        else:
            acc[si] += wv
            run[si] += rowsum[buf, slot]

    def diag_body(g, carry):
        for buf in range(2):
            for slot in range(per_trip):
                qi = (2 * g + buf) * per_trip + slot
                terms(qi, qi, buf, slot, True)
        for buf in range(2):
            for slot in range(per_trip):
                qi = (2 * g + buf) * per_trip + slot
                apply(qi, qi, buf, slot, True)
        return carry

    lax.fori_loop(0, n_qt // (2 * per_trip), diag_body, 0)
    run[n_qt] = jnp.zeros((tile, 1), F32)
    acc[n_qt] = jnp.zeros((tile, HEAD_DIM), F32)

    def trip_terms(t, buf):
        for slot in range(per_trip):
            f = t * per_trip + slot
            terms(jnp.minimum(qtab[f], n_qt - 1), ktab[f], buf, slot, False)

    def trip_apply(t, buf):
        for slot in range(per_trip):
            f = t * per_trip + slot
            apply(qtab[f], ktab[f], buf, slot, False)

    trip_terms(0, 0)

    def body(i, carry):
        t = 2 * i
        trip_terms(t + 1, 1)
        trip_apply(t, 0)
        trip_terms(t + 2, 0)
        trip_apply(t + 1, 1)
        return carry

    lax.fori_loop(0, n_trips // 2 - 1, body, 0)
    trip_terms(n_trips - 1, 1)
    trip_apply(n_trips - 2, 0)
    trip_apply(n_trips - 1, 1)
    o_ref[...] = acc[0:n_qt].reshape(n_qt * tile, HEAD_DIM).astype(o_ref.dtype)


def _sb_prompt_attention(q, k, v, logit_bias, *, n_seq, seq_len):
    t, d = q.shape
    n_heads = d // HEAD_DIM
    tile = ATT_TILE
    per_trip = ATT_TILES_PER_TRIP
    n_qt = seq_len // tile
    assert seq_len % tile == 0 and n_qt % (2 * per_trip) == 0
    tiles = [(qi, qi - dist) for dist in range(1, n_qt) for qi in range(dist, n_qt)]
    n_trips = max(2, 2 * -(-len(tiles) // (2 * per_trip)))
    tiles += [(n_qt, 0)] * (n_trips * per_trip - len(tiles))
    qtab = jnp.asarray([qi for qi, _ in tiles], jnp.int32)
    ktab = jnp.asarray([kj for _, kj in tiles], jnp.int32)
    suf = (jnp.arange(tile)[:, None] > jnp.arange(tile)[None, :]).astype(BF16)
    seq_spec = pl.BlockSpec((seq_len, HEAD_DIM), lambda b, h, qt, kt: (b, h))
    grid_spec = pltpu.PrefetchScalarGridSpec(
        num_scalar_prefetch=2,
        grid=(n_seq, n_heads),
        in_specs=[
            pl.BlockSpec(memory_space=pltpu.SMEM),
            seq_spec, seq_spec, seq_spec,
            pl.BlockSpec((tile, tile), lambda b, h, qt, kt: (0, 0)),
        ],
        out_specs=seq_spec,
        scratch_shapes=[
            pltpu.VMEM((2, per_trip, tile, tile), F32),
            pltpu.VMEM((2, per_trip, tile, tile), BF16),
            pltpu.VMEM((2, per_trip, tile, 1), F32),
            pltpu.VMEM((n_qt + 1, tile, 1), F32),
            pltpu.VMEM((n_qt + 1, tile, HEAD_DIM), F32),
        ],
    )
    return pl.pallas_call(
        functools.partial(_sb_wave_kernel, tile=tile, n_qt=n_qt, n_trips=n_trips),
        grid_spec=grid_spec,
        out_shape=jax.ShapeDtypeStruct((t, d), BF16),
        compiler_params=_params("parallel", "arbitrary"),
        name="sb_prompt_attention",
    )(qtab, ktab, logit_bias.astype(F32) * LOG2E, q, k, v, suf)


def _sb_sample_kernel(pt_ref, q_ref, kn_ref, vn_ref, *refs, n_heads, n_q, n_new, page, n_steps, per_step):
    kc_hbm, vc_hbm, bias_ref, scan_ref, o_ref, run, acc, kbuf, vbuf, sems = refs
    step = pl.program_id(1)
    n_pages = n_steps * per_step
    n_total = pl.num_programs(0) * n_steps
    g_now = pl.program_id(0) * n_steps + step

    def page_copies(g, slot):
        seq = g // n_steps
        first = (g - seq * n_steps) * per_step
        copies = []
        for i in range(per_step):
            pid = pt_ref[seq, n_pages - 1 - (first + i)]
            copies.append(pltpu.make_async_copy(kc_hbm.at[pid], kbuf.at[slot, i], sems.at[slot, 0]))
            copies.append(pltpu.make_async_copy(vc_hbm.at[pid], vbuf.at[slot, i], sems.at[slot, 1]))
        return copies

    def start_fetch(g):
        for cp in page_copies(g, lax.rem(g, SAMPLE_RING)):
            cp.start()

    @pl.when(g_now == 0)
    def _():
        for g in range(SAMPLE_RING - 1):
            @pl.when(g < n_total)
            def _():
                start_fetch(jnp.int32(g))

    @pl.when(g_now + (SAMPLE_RING - 1) < n_total)
    def _():
        start_fetch(g_now + (SAMPLE_RING - 1))

    slot_now = lax.rem(g_now, SAMPLE_RING)
    for cp in page_copies(g_now, slot_now):
        cp.wait()
    kc_refs = [kbuf.at[slot_now, i] for i in range(per_step)]
    vc_refs = [vbuf.at[slot_now, i] for i in range(per_step)]
    heads = range(n_heads)
    scan = scan_ref[...]
    qrow = lax.broadcasted_iota(jnp.int32, (SUBLANES, HEAD_DIM), 0)
    kcol = lax.broadcasted_iota(jnp.int32, (SUBLANES, HEAD_DIM), 1)
    live_q = qrow < n_q
    q = [q_ref[0, h].astype(BF16) for h in heads]

    def log_weights(keys, live):
        units = [(g, h) for g in range(len(keys)) for h in heads]
        z = [_dot_nt(q[h], keys[g][h]) + bias_ref[h] for g, h in units]
        lt = [_sb_terms2(zz) for zz in z]
        ls_neg = [jnp.where(live, t[1], 0.0) for t in lt]
        rel, tot = {}, {}
        for g in range(len(keys)):
            hi, lo = _split_bf16(jnp.concatenate(ls_neg[g * n_heads:(g + 1) * n_heads], axis=0))
            res = _dot(hi, scan) + _dot(lo, scan)
            for h in heads:
                blk = res[h * SUBLANES:(h + 1) * SUBLANES, :]
                rel[g, h] = lt[g * n_heads + h][0] + blk[:, 0:HEAD_DIM]
                tot[g, h] = blk[:, HEAD_DIM:]
        return rel, tot

    def weighted_values(rel, run_val, live, values):
        w = jnp.exp2(jnp.where(live, rel + run_val, -jnp.inf)).astype(BF16)
        return _dot(w, values)

    def strided_head(ref, h):
        return ref[pl.ds(h, page, stride=n_heads), :].astype(BF16)

    @pl.when(step == 0)
    def _():
        live = live_q & (kcol < qrow) & (kcol < n_new)
        zpad = jnp.zeros((HEAD_DIM - SUBLANES, HEAD_DIM), F32)
        pad = lambda ref, h: jnp.concatenate([ref[0, h], zpad], axis=0).astype(BF16)
        rel, tot = log_weights([[pad(kn_ref, h) for h in heads]], live)
        for h in heads:
            acc[h] = weighted_values(rel[0, h], jnp.zeros((SUBLANES, HEAD_DIM), F32), live, pad(vn_ref, h))
            run[h] = tot[0, h]

    rel, tot = log_weights([[strided_head(kc_refs[g], h) for h in heads] for g in range(per_step)], live_q)
    for h in heads:
        run_val = run[h]
        acc_val = acc[h]
        for g in range(per_step):
            acc_val = acc_val + weighted_values(rel[g, h], run_val, live_q, strided_head(vc_refs[g], h))
            run_val = run_val + tot[g, h]
        run[h] = run_val
        acc[h] = acc_val

    @pl.when(step == n_steps - 1)
    def _():
        o_ref[0] = acc[...]


def _sb_sample_attention(q, k_new, v_new, cache_k, cache_v, page_table, logit_bias, *, n_seq, n_q):
    t, d = q.shape
    n_heads = d // HEAD_DIM
    n_phys, page = cache_k.shape[0], cache_k.shape[1]
    n_pages = page_table.shape[1]
    assert page == HEAD_DIM and n_q <= SUBLANES
    by_head = lambda a: jnp.pad(a.astype(F32).reshape(n_seq, n_q, n_heads, HEAD_DIM).transpose(0, 2, 1, 3),
                                ((0, 0), (0, 0), (0, SUBLANES - n_q), (0, 0)))
    kc = cache_k.reshape(n_phys, page * n_heads, HEAD_DIM)
    vc = cache_v.reshape(n_phys, page * n_heads, HEAD_DIM)
    key = jnp.arange(HEAD_DIM)
    scan = jnp.concatenate([key[:, None] > key[None, :], jnp.ones((HEAD_DIM, HEAD_DIM), bool)], axis=1).astype(BF16)
    per_step = math.gcd(n_pages, SAMPLE_PAGES_PER_STEP)
    n_steps = n_pages // per_step
    seq4 = lambda b, j, pt: (b, 0, 0, 0)
    head_spec = pl.BlockSpec((1, n_heads, SUBLANES, HEAD_DIM), seq4)
    ring = pltpu.VMEM((SAMPLE_RING, per_step, page * n_heads, HEAD_DIM), F32)
    grid_spec = pltpu.PrefetchScalarGridSpec(
        num_scalar_prefetch=1,
        grid=(n_seq, n_steps),
        in_specs=[head_spec] * 3 + [pl.BlockSpec(memory_space=pl.ANY)] * 2 + [
            pl.BlockSpec(memory_space=pltpu.SMEM),
            pl.BlockSpec((HEAD_DIM, 2 * HEAD_DIM), lambda b, j, pt: (0, 0)),
        ],
        out_specs=head_spec,
        scratch_shapes=[pltpu.VMEM((n_heads, SUBLANES, HEAD_DIM), F32)] * 2 + [
            ring, ring, pltpu.SemaphoreType.DMA((SAMPLE_RING, 2))],
    )
    out = pl.pallas_call(
        functools.partial(_sb_sample_kernel, n_heads=n_heads, n_q=n_q, n_new=n_q, page=page, n_steps=n_steps,
                          per_step=per_step),
        grid_spec=grid_spec,
        out_shape=jax.ShapeDtypeStruct((n_seq, n_heads, SUBLANES, HEAD_DIM), F32),
        compiler_params=_params("arbitrary", "arbitrary"),
        name="sb_sample_attention",
    )(page_table, by_head(q), by_head(k_new), by_head(v_new), kc, vc, logit_bias.astype(F32) * LOG2E, scan)
    return out[:, :, :n_q].transpose(0, 2, 1, 3).reshape(t, d).astype(BF16)


def _trunk(x, tail, s0, past, w, *, n_seq, seq_len):
    d = x.shape[1]
    x = _ffn(x, w["ffn1_norm"], w["ffn1_w_gu"], w["ffn1_w_down"], 0)
    proj = _normproj(x, w["mix_norm"][0], w["gdn_w_in"])
    seq_rows = -(-seq_len // SUBLANES) * SUBLANES
    chunk = min(GDN_CHUNK, seq_rows)
    pad = lambda a: jnp.pad(a.reshape(n_seq, seq_len, -1), ((0, 0), (0, seq_rows - seq_len), (0, 0))
                            ).reshape(n_seq * seq_rows, -1)
    xg, pg = (x, proj) if seq_rows == seq_len else (pad(x), pad(proj))
    xg, state = _gdn_layer(xg, pg, tail, s0, w["gdn_conv_w"], w["gdn_a_log"], w["gdn_dt_bias"], w["gdn_o_norm"],
                           w["gdn_w_out"], n_seq=n_seq, seq_rows=seq_rows, chunk=chunk,
                           valid_rows=min(seq_len, chunk))
    x = xg if seq_rows == seq_len else xg.reshape(n_seq, seq_rows, d)[:, :seq_len].reshape(n_seq * seq_len, d)
    x = _ffn(x, w["ffn2_norm"], w["ffn2_w_gu"], w["ffn2_w_down"], 0)
    k, v, kb, vb = _shared_kv(x, w["kv_norm"], w["w_kv"], w["k_norm"])
    x, q = _ffn(x, w["ffn1_norm"], w["ffn1_w_gu"], w["ffn1_w_down"], 1,
                q_proj=(w["mix_norm"][1], w["sb_w_q"], w["sb_q_norm"]))
    if past is None:
        att = _sb_prompt_attention(q, kb, vb, w["sb_logit_bias"], n_seq=n_seq, seq_len=seq_len)
    else:
        att = _sb_sample_attention(q, kb, vb, past[0], past[1], past[2], w["sb_logit_bias"], n_seq=n_seq, n_q=seq_len)
    x = _ffn(x, w["ffn2_norm"], w["ffn2_w_gu"], w["ffn2_w_down"], 1, mixer_out=(att, w["sb_w_out"]))
    return x, proj.reshape(n_seq, seq_len, -1), state, k, v


def kernel(x_prompt, x_sample, state_gdn, state_conv, cache_k, cache_v, page_table, ffn1_norm, ffn1_w_gu,
           ffn1_w_down, ffn2_norm, ffn2_w_gu, ffn2_w_down, mix_norm, gdn_w_in, gdn_conv_w, gdn_a_log, gdn_dt_bias,
           gdn_o_norm, gdn_w_out, kv_norm, w_kv, k_norm, sb_w_q, sb_q_norm, sb_logit_bias, sb_w_out):
    bp, seq, d = x_prompt.shape
    bs, dec = x_sample.shape[0], x_sample.shape[1]
    n_heads = d // HEAD_DIM
    hist = state_conv.shape[2]
    assert gdn_w_in.shape[0] == 1 and sb_w_q.shape[0] == 1, "one layer of each mixer kind"
    assert dec <= SUBLANES and hist < SUBLANES and dec >= hist

    in_cols = gdn_w_in.shape[2]
    np_cols = -(-in_cols // HEAD_DIM) * HEAD_DIM
    w = {
        "ffn1_norm": ffn1_norm, "ffn2_norm": ffn2_norm, "mix_norm": mix_norm,
        "ffn1_w_gu": ffn1_w_gu.astype(BF16), "ffn1_w_down": ffn1_w_down.astype(BF16),
        "ffn2_w_gu": ffn2_w_gu.astype(BF16), "ffn2_w_down": ffn2_w_down.astype(BF16),
        "gdn_w_in": jnp.pad(gdn_w_in[0], ((0, 0), (0, np_cols - in_cols))).astype(BF16),
        "gdn_conv_w": gdn_conv_w[0], "gdn_a_log": gdn_a_log[0], "gdn_dt_bias": gdn_dt_bias[0],
        "gdn_o_norm": gdn_o_norm[0], "gdn_w_out": gdn_w_out[0].astype(BF16),
        "kv_norm": kv_norm, "w_kv": w_kv.astype(BF16), "k_norm": k_norm,
        "sb_w_q": sb_w_q[0].astype(BF16), "sb_q_norm": sb_q_norm[0], "sb_logit_bias": sb_logit_bias[0],
        "sb_w_out": sb_w_out[0].astype(BF16),
    }

    zero_tail = jnp.zeros((bp, SUBLANES, 3 * d), F32)
    zero_state = jnp.zeros((bp, n_heads, HEAD_DIM, HEAD_DIM), F32)
    y_p, qkv_p, st_p, k_p, v_p = _trunk(x_prompt.reshape(bp * seq, d), zero_tail, zero_state, None, w,
                                        n_seq=bp, seq_len=seq)

    tail_s = jnp.pad(state_conv[0], ((0, 0), (SUBLANES - hist, 0), (0, 0)))
    y_s, qkv_s, st_s, k_s, v_s = _trunk(x_sample.reshape(bs * dec, d), tail_s, state_gdn[0],
                                        (cache_k, cache_v, page_table), w, n_seq=bs, seq_len=dec)

    conv_p = qkv_p[:, seq - hist:seq, :3 * d][None]
    conv_s = qkv_s[:, dec - hist:dec, :3 * d][None]
    return (y_p.reshape(bp, seq, d), y_s.reshape(bs, dec, d), st_p[None].astype(x_prompt.dtype), conv_p,
            k_p.reshape(bp, seq, n_heads, HEAD_DIM), v_p.reshape(bp, seq, n_heads, HEAD_DIM),
            st_s[None].astype(x_sample.dtype), conv_s,
            k_s.reshape(bs, dec, n_heads, HEAD_DIM), v_s.reshape(bs, dec, n_heads, HEAD_DIM))
```
